```python
import jax, jax.numpy as jnp
from jax import lax
import numpy as np

D_MODEL = 2048
BATCH = 8
SEQ = 4096
DEPTH = 4

SSD_EXPAND = 2
SSD_D_INNER = SSD_EXPAND * D_MODEL
SSD_HEAD_DIM = 64
SSD_N_HEADS = SSD_D_INNER // SSD_HEAD_DIM
SSD_N_GROUPS = 8
SSD_HEADS_PER_GROUP = SSD_N_HEADS // SSD_N_GROUPS
SSD_D_STATE = 128
SSD_CONV = 4
SSD_CHUNK = 128
SSD_BC_DIM = SSD_N_GROUPS * SSD_D_STATE
SSD_CONV_DIM = SSD_D_INNER + 2 * SSD_BC_DIM
SSD_IN_DIM = SSD_D_INNER + SSD_CONV_DIM + SSD_N_HEADS

GLA_N_HEADS = 4
GLA_DK = D_MODEL // 2
GLA_DV = D_MODEL
GLA_HK = GLA_DK // GLA_N_HEADS
GLA_HV = GLA_DV // GLA_N_HEADS
GLA_GATE_RANK = 16
GLA_GATE_TAU = 16.0
GLA_CHUNK = 64
GLA_IN_DIM = 2 * GLA_DK + 2 * GLA_DV + GLA_GATE_RANK

MOE_GROUPS = 8
MOE_EXPERTS_PER_GROUP = 8
MOE_N_EXPERTS = MOE_GROUPS * MOE_EXPERTS_PER_GROUP
MOE_TOP_K = 2
MOE_D_FF = 256
MOE_BLOCK = 128

N_SSD_LAYERS = (DEPTH + 1) // 2
N_GLA_LAYERS = DEPTH // 2
EPS = 1e-6

kernel_name = "hybrid_ssd_gla_hier_moe"


def rms_norm(x, w):
    xf = x.astype(jnp.float32)
    y = xf * lax.rsqrt(jnp.mean(xf * xf, axis=-1, keepdims=True) + EPS)
    return (y * w.astype(jnp.float32)).astype(x.dtype)


def to_chunks(t, c):
    b, tl = t.shape[0], t.shape[1]
    t = t.reshape((b, tl // c, c) + t.shape[2:])
    return jnp.moveaxis(t, 1, 0)


def from_chunks(t):
    t = jnp.moveaxis(t, 0, 1)
    return t.reshape((t.shape[0], t.shape[1] * t.shape[2]) + t.shape[3:])


def causal_depthwise_conv(u, w, b):
    ch = u.shape[-1]
    out = lax.conv_general_dilated(
        u, w[:, None, :].astype(u.dtype), window_strides=(1,),
        padding=[(SSD_CONV - 1, 0)], dimension_numbers=("NWC", "WIO", "NWC"),
        feature_group_count=ch)
    return out + b.astype(u.dtype)


def ssd_chunk_scan(xh, dt, a, bm, cm):
    bsz = xh.shape[0]
    causal = jnp.tril(jnp.ones((SSD_CHUNK, SSD_CHUNK), dtype=bool))[None, :, :, None, None]

    def step(state, inp):
        xc, dtc, bc, cc = inp
        acs = jnp.cumsum(dtc * a, axis=1)
        seg = acs[:, :, None] - acs[:, None, :]
        decay = jnp.exp(jnp.where(causal, seg, -jnp.inf))
        cb = jnp.einsum("blgn,bsgn->blsg", cc, bc)
        xdt = xc * dtc[..., None]
        y_diag = jnp.einsum("blsgh,bsghp->blghp", cb[..., None] * decay, xdt)
        y_off = jnp.einsum("blgn,bghpn->blghp", cc, state) * jnp.exp(acs)[..., None]
        to_end = jnp.exp(acs[:, -1:] - acs)
        state = (state * jnp.exp(acs[:, -1])[..., None, None]
                 + jnp.einsum("bsgn,bsgh,bsghp->bghpn", bc, to_end, xdt))
        return state, y_diag + y_off

    state0 = jnp.zeros((bsz, SSD_N_GROUPS, SSD_HEADS_PER_GROUP, SSD_HEAD_DIM, SSD_D_STATE), jnp.float32)
    _, y = lax.scan(step, state0, (to_chunks(xh, SSD_CHUNK), to_chunks(dt, SSD_CHUNK),
                                   to_chunks(bm, SSD_CHUNK), to_chunks(cm, SSD_CHUNK)))
    return from_chunks(y)


def ssd_mixer(h, w_in, conv_w, conv_b, dt_bias, a_log, d_skip, norm_w, w_out):
    bsz, tl, _ = h.shape
    f32 = jnp.float32
    proj = h @ w_in
    z, xbc, dt = jnp.split(proj, [SSD_D_INNER, SSD_D_INNER + SSD_CONV_DIM], axis=-1)
    xbc = jax.nn.silu(causal_depthwise_conv(xbc, conv_w, conv_b))
    xs, bm, cm = jnp.split(xbc, [SSD_D_INNER, SSD_D_INNER + SSD_BC_DIM], axis=-1)
    xh = xs.astype(f32).reshape(bsz, tl, SSD_N_GROUPS, SSD_HEADS_PER_GROUP, SSD_HEAD_DIM)
    bm = bm.astype(f32).reshape(bsz, tl, SSD_N_GROUPS, SSD_D_STATE)
    cm = cm.astype(f32).reshape(bsz, tl, SSD_N_GROUPS, SSD_D_STATE)
    dt = jax.nn.softplus(dt.astype(f32) + dt_bias.astype(f32)).reshape(
        bsz, tl, SSD_N_GROUPS, SSD_HEADS_PER_GROUP)
    a = -jnp.exp(a_log.astype(f32)).reshape(SSD_N_GROUPS, SSD_HEADS_PER_GROUP)
    y = ssd_chunk_scan(xh, dt, a, bm, cm)
    y = y + xh * d_skip.astype(f32).reshape(SSD_N_GROUPS, SSD_HEADS_PER_GROUP)[..., None]
    gsz = SSD_D_INNER // SSD_N_GROUPS
    y = y.reshape(bsz, tl, SSD_N_GROUPS, gsz) * jax.nn.silu(z.astype(f32)).reshape(bsz, tl, SSD_N_GROUPS, gsz)
    y = y * lax.rsqrt(jnp.mean(y * y, axis=-1, keepdims=True) + EPS)
    y = y.reshape(bsz, tl, SSD_D_INNER) * norm_w.astype(f32)
    return y.astype(h.dtype) @ w_out


def gla_chunk_scan(q, k, v, lg):
    bsz = q.shape[0]
    causal = jnp.tril(jnp.ones((GLA_CHUNK, GLA_CHUNK), dtype=bool))[None, :, :, None, None]

    def step(state, inp):
        qc, kc, vc, gc = inp
        bcs = jnp.cumsum(gc, axis=1)
        seg = bcs[:, :, None] - bcs[:, None, :]
        decay = jnp.exp(jnp.where(causal, seg, -jnp.inf))
        scores = jnp.einsum("blhk,blshk,bshk->bhls", qc, decay, kc)
        o = (jnp.einsum("bhls,bshv->blhv", scores, vc)
             + jnp.einsum("blhk,bhkv->blhv", qc * jnp.exp(bcs), state))
        last = bcs[:, -1]
        state = (state * jnp.exp(last)[..., None]
                 + jnp.einsum("bshk,bshv->bhkv", kc * jnp.exp(last[:, None] - bcs), vc))
        return state, o

    state0 = jnp.zeros((bsz, GLA_N_HEADS, GLA_HK, GLA_HV), jnp.float32)
    _, o = lax.scan(step, state0, (to_chunks(q, GLA_CHUNK), to_chunks(k, GLA_CHUNK),
                                   to_chunks(v, GLA_CHUNK), to_chunks(lg, GLA_CHUNK)))
    return from_chunks(o)


def gla_mixer(h, w_in, w_a2, b_a, norm_w, w_out):
    bsz, tl, _ = h.shape
    f32 = jnp.float32
    proj = h @ w_in
    q, k, v, g, a_lr = jnp.split(
        proj, [GLA_DK, 2 * GLA_DK, 2 * GLA_DK + GLA_DV, 2 * GLA_DK + 2 * GLA_DV], axis=-1)
    lg = jax.nn.log_sigmoid((a_lr @ w_a2 + b_a).astype(f32)) / GLA_GATE_TAU
    q = q.astype(f32).reshape(bsz, tl, GLA_N_HEADS, GLA_HK) * (GLA_HK ** -0.5)
    k = k.astype(f32).reshape(bsz, tl, GLA_N_HEADS, GLA_HK)
    v = v.astype(f32).reshape(bsz, tl, GLA_N_HEADS, GLA_HV)
    lg = lg.reshape(bsz, tl, GLA_N_HEADS, GLA_HK)
    o = gla_chunk_scan(q, k, v, lg)
    o = o * lax.rsqrt(jnp.mean(o * o, axis=-1, keepdims=True) + EPS) * norm_w.astype(f32)
    o = o.reshape(bsz, tl, GLA_DV) * jax.nn.silu(g.astype(f32))
    return o.astype(h.dtype) @ w_out


def hier_moe(h, w_group, b_group, w_expert, b_expert, w_gate, w_up, w_down):
    bsz, tl, d = h.shape
    n_tok = bsz * tl
    f32 = jnp.float32
    hf = h.reshape(n_tok, d)
    grp_p = jax.nn.softmax((hf @ w_group + b_group).astype(f32), axis=-1)
    grp_w, grp_i = lax.top_k(grp_p, 1)
    exp_logits = (hf @ w_expert + b_expert).astype(f32).reshape(n_tok, MOE_GROUPS, MOE_EXPERTS_PER_GROUP)
    sel_logits = jnp.take_along_axis(exp_logits, grp_i[:, :, None], axis=1)[:, 0]
    top_w, top_i = lax.top_k(jax.nn.softmax(sel_logits, axis=-1), MOE_TOP_K)
    gate = grp_w * top_w / jnp.sum(top_w, axis=-1, keepdims=True)
    expert_id = (grp_i * MOE_EXPERTS_PER_GROUP + top_i).reshape(-1)
    tok_id = jnp.repeat(jnp.arange(n_tok, dtype=jnp.int32), MOE_TOP_K)
    gate_flat = gate.reshape(-1)
    order = jnp.argsort(expert_id)
    e_s, t_s, g_s = expert_id[order], tok_id[order], gate_flat[order]
    counts = jnp.zeros((MOE_N_EXPERTS,), jnp.int32).at[expert_id].add(1)
    offsets = jnp.cumsum(counts) - counts
    pcounts = (counts + MOE_BLOCK - 1) // MOE_BLOCK * MOE_BLOCK
    pends = jnp.cumsum(pcounts)
    poffs = pends - pcounts
    n_assign = n_tok * MOE_TOP_K
    n_blocks = -(-n_assign // MOE_BLOCK) + MOE_N_EXPERTS
    dest = poffs[e_s] + jnp.arange(n_assign, dtype=jnp.int32) - offsets[e_s]
    x_buf = jnp.zeros((n_blocks * MOE_BLOCK, d), h.dtype).at[dest].set(hf[t_s])
    block_e = jnp.minimum(
        jnp.searchsorted(pends, jnp.arange(n_blocks, dtype=jnp.int32) * MOE_BLOCK, side="right"),
        MOE_N_EXPERTS - 1)

    def expert_block(args):
        xb, e = args
        return (jax.nn.silu(xb @ w_gate[e]) * (xb @ w_up[e])) @ w_down[e]

    y_buf = lax.map(expert_block, (x_buf.reshape(n_blocks, MOE_BLOCK, d), block_e))
    y_buf = y_buf.reshape(n_blocks * MOE_BLOCK, d)
    y = jnp.zeros((n_tok, d), h.dtype).at[t_s].add(y_buf[dest] * g_s[:, None].astype(h.dtype))
    return y.reshape(bsz, tl, d)


def setup_inputs(seed: int = 0) -> dict:
    key = jax.random.key(seed)
    ks = jax.random.split(key, 32)
    f32 = jnp.float32
    nrm = lambda k, shape, scale: jax.random.normal(k, shape, f32) * scale
    ns, ng, L = N_SSD_LAYERS, N_GLA_LAYERS, DEPTH
    dt0 = jnp.exp(jax.random.uniform(ks[3], (ns, SSD_N_HEADS), f32, np.log(1e-3), np.log(1e-1)))
    return {
        "x": nrm(ks[0], (BATCH, SEQ, D_MODEL), 1.0),
        "ssd_w_in": nrm(ks[1], (ns, D_MODEL, SSD_IN_DIM), D_MODEL ** -0.5),
        "ssd_conv_w": nrm(ks[2], (ns, SSD_CONV, SSD_CONV_DIM), SSD_CONV ** -0.5),
        "ssd_conv_b": nrm(ks[4], (ns, SSD_CONV_DIM), 0.02),
        "ssd_dt_bias": dt0 + jnp.log(-jnp.expm1(-dt0)),
        "ssd_a_log": jnp.log(jax.random.uniform(ks[5], (ns, SSD_N_HEADS), f32, 1.0, 16.0)),
        "ssd_d": 1.0 + nrm(ks[6], (ns, SSD_N_HEADS), 0.02),
        "ssd_norm_w": 1.0 + nrm(ks[7], (ns, SSD_D_INNER), 0.02),
        "ssd_w_out": nrm(ks[8], (ns, SSD_D_INNER, D_MODEL), SSD_D_INNER ** -0.5),
        "gla_w_in": nrm(ks[9], (ng, D_MODEL, GLA_IN_DIM), D_MODEL ** -0.5),
        "gla_w_a2": nrm(ks[10], (ng, GLA_GATE_RANK, GLA_DK), GLA_GATE_RANK ** -0.5),
        "gla_b_a": nrm(ks[11], (ng, GLA_DK), 0.02),
        "gla_norm_w": 1.0 + nrm(ks[12], (ng, GLA_HV), 0.02),
        "gla_w_out": nrm(ks[13], (ng, GLA_DV, D_MODEL), GLA_DV ** -0.5),
        "norm_mix": 1.0 + nrm(ks[14], (L, D_MODEL), 0.02),
        "norm_ffn": 1.0 + nrm(ks[15], (L, D_MODEL), 0.02),
        "moe_w_group": nrm(ks[16], (L, D_MODEL, MOE_GROUPS), D_MODEL ** -0.5),
        "moe_b_group": nrm(ks[17], (L, MOE_GROUPS), 0.01),
        "moe_w_expert": nrm(ks[18], (L, D_MODEL, MOE_N_EXPERTS), D_MODEL ** -0.5),
        "moe_b_expert": nrm(ks[19], (L, MOE_N_EXPERTS), 0.01),
        "moe_w_gate": nrm(ks[20], (L, MOE_N_EXPERTS, D_MODEL, MOE_D_FF), D_MODEL ** -0.5),
        "moe_w_up": nrm(ks[21], (L, MOE_N_EXPERTS, D_MODEL, MOE_D_FF), D_MODEL ** -0.5),
        "moe_w_down": nrm(ks[22], (L, MOE_N_EXPERTS, MOE_D_FF, D_MODEL), MOE_D_FF ** -0.5),
        "norm_final": 1.0 + nrm(ks[23], (D_MODEL,), 0.02),
    }


def reference(x, ssd_w_in, ssd_conv_w, ssd_conv_b, ssd_dt_bias, ssd_a_log, ssd_d, ssd_norm_w,
              ssd_w_out, gla_w_in, gla_w_a2, gla_b_a, gla_norm_w, gla_w_out, norm_mix, norm_ffn,
              moe_w_group, moe_b_group, moe_w_expert, moe_b_expert, moe_w_gate, moe_w_up,
              moe_w_down, norm_final):
    for i in range(DEPTH):
        j = i // 2
        h = rms_norm(x, norm_mix[i])
        if i % 2 == 0:
            x = x + ssd_mixer(h, ssd_w_in[j], ssd_conv_w[j], ssd_conv_b[j], ssd_dt_bias[j],
                              ssd_a_log[j], ssd_d[j], ssd_norm_w[j], ssd_w_out[j])
        else:
            x = x + gla_mixer(h, gla_w_in[j], gla_w_a2[j], gla_b_a[j], gla_norm_w[j], gla_w_out[j])
        h = rms_norm(x, norm_ffn[i])
        x = x + hier_moe(h, moe_w_group[i], moe_b_group[i], moe_w_expert[i], moe_b_expert[i],
                         moe_w_gate[i], moe_w_up[i], moe_w_down[i])
    return rms_norm(x, norm_final)
```

```python
import functools

import numpy as np
import jax
import jax.numpy as jnp
from jax import lax
from jax.experimental import pallas as pl
from jax.experimental.pallas import tpu as pltpu

F32 = jnp.float32
BF16 = jnp.bfloat16
U32 = jnp.uint32
I32 = jnp.int32

EPS = 1e-6
LANES = 128
VMEM_LIMIT = 56 * 1024 * 1024

SSD_HEAD_DIM = 64
SSD_HEADS_PER_GROUP = 8
SSD_D_STATE = 128
SSD_CONV = 4
SSD_CHUNK = 128
SSD_GROUP_W = SSD_HEAD_DIM * SSD_HEADS_PER_GROUP

GLA_N_HEADS = 4
GLA_GATE_RANK = 16
GLA_GATE_TAU = 16.0
GLA_CHUNK = 64
GLA_LEVELS = 6

MOE_GROUPS = 8
MOE_EXPERTS_PER_GROUP = 8
MOE_N_EXPERTS = 64
MOE_TOP_K = 2
MOE_ROWS = 256


def _cparams(sem):
    return pltpu.CompilerParams(dimension_semantics=sem, vmem_limit_bytes=VMEM_LIMIT)


def _split_hi_lo(v):
    hi = v.astype(BF16)
    lo = (v - hi.astype(F32)).astype(BF16)
    return hi, lo


def _dot(a, b):
    return jnp.dot(a, b, preferred_element_type=F32)


def _dot_tb(a, b):
    return lax.dot_general(a, b, (((1,), (1,)), ((), ())), preferred_element_type=F32)


def _dot_ta(a, b):
    return lax.dot_general(a, b, (((0,), (0,)), ((), ())), preferred_element_type=F32)


def _silu(v):
    return v * (1.0 / (1.0 + jnp.exp(-v)))


def _pack_bf16_pair(a, b):
    ua = lax.bitcast_convert_type(a.astype(BF16).astype(F32), U32)
    ub = lax.bitcast_convert_type(b.astype(BF16).astype(F32), U32)
    return ua | (ub >> 16)


def _unpack_bf16_pair(u):
    a = lax.bitcast_convert_type(u & jnp.uint32(0xFFFF0000), F32)
    b = lax.bitcast_convert_type(u << 16, F32)
    return a, b


def _norm_matmul_kernel(x_ref, nw_ref, w_ref, ws_ref, o_ref, os_ref, h_ref):
    j = pl.program_id(1)

    @pl.when(j == 0)
    def _():
        x = x_ref[...]
        h = x * lax.rsqrt(jnp.mean(x * x, axis=-1, keepdims=True) + EPS) * nw_ref[...]
        hi, lo = _split_hi_lo(h)
        h_ref[...] = hi
        s = _dot(hi, ws_ref[...])
        os_ref[...] = s[:, :LANES] + s[:, LANES:] + _dot(lo, ws_ref[:, :LANES])

    o_ref[...] = _dot(h_ref[...], w_ref[...]).astype(o_ref.dtype)


def _small_weight(w):
    k, n = w.shape
    wp = jnp.zeros((k, LANES), F32).at[:, :n].set(w)
    hi, lo = _split_hi_lo(wp)
    return jnp.concatenate([hi, lo], axis=1)


def norm_matmul(x, nw, w_main, w_small, tm, tn):
    n, d = x.shape
    m = w_main.shape[1]
    return pl.pallas_call(
        _norm_matmul_kernel,
        grid=(n // tm, m // tn),
        in_specs=[
            pl.BlockSpec((tm, d), lambda i, j: (i, 0)),
            pl.BlockSpec((1, d), lambda i, j: (0, 0)),
            pl.BlockSpec((d, tn), lambda i, j: (0, j)),
            pl.BlockSpec((d, 2 * LANES), lambda i, j: (0, 0)),
        ],
        out_specs=[
            pl.BlockSpec((tm, tn), lambda i, j: (i, j)),
            pl.BlockSpec((tm, LANES), lambda i, j: (i, 0)),
        ],
        out_shape=[
            jax.ShapeDtypeStruct((n, m), BF16),
            jax.ShapeDtypeStruct((n, LANES), F32),
        ],
        scratch_shapes=[pltpu.VMEM((tm, d), BF16)],
        compiler_params=_cparams(("parallel", "arbitrary")),
        name="norm_matmul",
    )(x, nw.reshape(1, d), w_main, w_small)


def _out_proj_kernel(y_ref, w_ref, x_ref, nw_ref, wr_ref, xo_ref, hp_ref, lg_ref):
    xn = x_ref[...] + _dot(y_ref[...], w_ref[...])
    xo_ref[...] = xn
    h = xn * lax.rsqrt(jnp.mean(xn * xn, axis=-1, keepdims=True) + EPS) * nw_ref[...]
    half = h.shape[1] // 2
    hp_ref[...] = _pack_bf16_pair(h[:, :half], h[:, half:])
    hi, lo = _split_hi_lo(h)
    s = _dot(hi, wr_ref[...])
    lg_ref[...] = s[:, :LANES] + s[:, LANES:] + _dot(lo, wr_ref[:, :LANES])


def out_proj(y, w_out, x, nw, w_router, tm):
    n, k = y.shape
    d = x.shape[1]
    return pl.pallas_call(
        _out_proj_kernel,
        grid=(n // tm,),
        in_specs=[
            pl.BlockSpec((tm, k), lambda i: (i, 0)),
            pl.BlockSpec((k, d), lambda i: (0, 0)),
            pl.BlockSpec((tm, d), lambda i: (i, 0)),
            pl.BlockSpec((1, d), lambda i: (0, 0)),
            pl.BlockSpec((d, 2 * LANES), lambda i: (0, 0)),
        ],
        out_specs=[
            pl.BlockSpec((tm, d), lambda i: (i, 0)),
            pl.BlockSpec((tm, d // 2), lambda i: (i, 0)),
            pl.BlockSpec((tm, LANES), lambda i: (i, 0)),
        ],
        out_shape=[
            jax.ShapeDtypeStruct((n, d), F32),
            jax.ShapeDtypeStruct((n, d // 2), U32),
            jax.ShapeDtypeStruct((n, LANES), F32),
        ],
        compiler_params=_cparams(("parallel",)),
        name="out_proj",
    )(y, w_out, x, nw.reshape(1, d), w_router)


def _ssd_kernel(z_ref, xs_ref, b_ref, c_ref, dt_ref, dtt_ref,
                cwx_ref, cwb_ref, cwc_ref, cbx_ref, cbb_ref, cbc_ref,
                bias_ref, biast_ref, a_ref, at_ref, dsk_ref, nw_ref,
                y_ref, state_ref, ubuf_ref, *, n_chunks):
    L = SSD_CHUNK
    W = SSD_GROUP_W
    NS = SSD_D_STATE
    HG = SSD_HEADS_PER_GROUP
    P = SSD_HEAD_DIM
    TAIL = 8
    WC = W + 2 * NS

    @pl.when(pl.program_id(2) == 0)
    def _():
        state_ref[...] = jnp.zeros_like(state_ref)
        ubuf_ref[0:TAIL, :] = jnp.zeros((TAIL, WC), F32)

    row = lax.broadcasted_iota(I32, (L, L), 0)
    col = lax.broadcasted_iota(I32, (L, L), 1)
    causal = row >= col
    tri = jnp.where(causal, 1.0, 0.0).astype(BF16)
    trit = jnp.where(col >= row, 1.0, 0.0).astype(BF16)
    er = lax.broadcasted_iota(I32, (2 * HG, W), 0)
    ec = lax.broadcasted_iota(I32, (2 * HG, W), 1)
    expand = jnp.where((er % HG) == (ec // P), 1.0, 0.0).astype(BF16)

    cw = jnp.concatenate([cwx_ref[...], cwb_ref[...], cwc_ref[...]], axis=1)
    cb = jnp.concatenate([cbx_ref[...], cbb_ref[...], cbc_ref[...]], axis=1)
    a_row = a_ref[...]
    a_col = at_ref[...]

    def expand_heads(v):
        hi, lo = _split_hi_lo(v)
        return _dot(jnp.concatenate([hi, lo], axis=1), expand)

    def chunk(ci, carry):
        r0 = pl.multiple_of(ci * L, L)
        rows = pl.ds(r0, L)
        u = jnp.concatenate([xs_ref[rows, :], b_ref[rows, :], c_ref[rows, :]], axis=1).astype(F32)
        ubuf_ref[TAIL:TAIL + L, :] = u
        acc = cb
        for k in range(SSD_CONV):
            off = TAIL - (SSD_CONV - 1) + k
            acc = acc + ubuf_ref[off:off + L, :] * cw[k:k + 1, :]
        ubuf_ref[0:TAIL, :] = u[L - TAIL:, :]
        conv = _silu(acc)
        xc = conv[:, :W]
        bc = conv[:, W:W + NS].astype(BF16)
        cc = conv[:, W + NS:].astype(BF16)

        dt = jax.nn.softplus(dt_ref[rows, :] + bias_ref[...])
        da_hi, da_lo = _split_hi_lo(dt * a_row)
        acs2 = _dot(tri, jnp.concatenate([da_hi, da_lo], axis=1))
        acs = acs2[:, :HG] + acs2[:, HG:]
        dtt = jax.nn.softplus(dtt_ref[:, rows] + biast_ref[...])
        dat_hi, dat_lo = _split_hi_lo(dtt * a_col)
        acst2 = _dot(jnp.concatenate([dat_hi, dat_lo], axis=0), trit)
        acst = acst2[:HG, :] + acst2[HG:, :]

        acs_last = acs[L - 1:L, :]
        dt_e = expand_heads(dt)
        eacs_e = expand_heads(jnp.exp(acs))
        wend_e = expand_heads(dt * jnp.exp(acs_last - acs))
        xdt = xc * dt_e
        xdt_b = xdt.astype(BF16)

        cbm = _dot_tb(cc, bc)
        state = state_ref[...]
        y_off = _dot(cc, state.astype(BF16)) * eacs_e
        ys = []
        for h in range(HG):
            seg = acs[:, h:h + 1] - acst[h:h + 1, :]
            m = jnp.where(causal, jnp.exp(seg), 0.0) * cbm
            ys.append(_dot(m.astype(BF16), xdt_b[:, h * P:(h + 1) * P]))
        y = jnp.concatenate(ys, axis=1) + y_off + xc * dsk_ref[...]
        state_ref[...] = state * eacs_e[L - 1:L, :] + _dot_ta(bc, (xc * wend_e).astype(BF16))

        yg = y * _silu(z_ref[rows, :].astype(F32))
        yn = yg * lax.rsqrt(jnp.mean(yg * yg, axis=-1, keepdims=True) + EPS) * nw_ref[...]
        y_ref[rows, :] = yn.astype(y_ref.dtype)
        return carry

    lax.fori_loop(0, n_chunks, chunk, 0)


def ssd_scan(proj, dtraw, conv_w, conv_b, dt_bias, a_log, d_skip, norm_w, bsz, tlen, tb):
    n = bsz * tlen
    n_heads = dt_bias.shape[0]
    g = n_heads // SSD_HEADS_PER_GROUP
    di = g * SSD_GROUP_W
    W, NS, HG = SSD_GROUP_W, SSD_D_STATE, SSD_HEADS_PER_GROUP
    nt = tlen // tb
    dt_g = dtraw[:, :n_heads].reshape(n, g, HG).transpose(1, 0, 2)
    dt_gt = dtraw[:, :n_heads].reshape(n, g, HG).transpose(1, 2, 0)
    bias = dt_bias.reshape(g, 1, HG)
    biast = dt_bias.reshape(g, HG, 1)
    a = -jnp.exp(a_log.astype(F32))
    a_row = a.reshape(g, 1, HG)
    a_col = a.reshape(g, HG, 1)
    dsk = jnp.repeat(d_skip.astype(F32), SSD_HEAD_DIM).reshape(1, di)
    cb = conv_b.reshape(1, -1)
    xoff = di // W
    boff = 2 * di // NS
    coff = boff + g
    rowblk = lambda b, gi, t: b * nt + t
    kern = functools.partial(_ssd_kernel, n_chunks=tb // SSD_CHUNK)
    return pl.pallas_call(
        kern,
        grid=(bsz, g, nt),
        in_specs=[
            pl.BlockSpec((tb, W), lambda b, gi, t: (rowblk(b, gi, t), gi)),
            pl.BlockSpec((tb, W), lambda b, gi, t: (rowblk(b, gi, t), xoff + gi)),
            pl.BlockSpec((tb, NS), lambda b, gi, t: (rowblk(b, gi, t), boff + gi)),
            pl.BlockSpec((tb, NS), lambda b, gi, t: (rowblk(b, gi, t), coff + gi)),
            pl.BlockSpec((None, tb, HG), lambda b, gi, t: (gi, rowblk(b, gi, t), 0)),
            pl.BlockSpec((None, HG, tb), lambda b, gi, t: (gi, 0, rowblk(b, gi, t))),
            pl.BlockSpec((SSD_CONV, W), lambda b, gi, t: (0, gi)),
            pl.BlockSpec((SSD_CONV, NS), lambda b, gi, t: (0, di // NS + gi)),
            pl.BlockSpec((SSD_CONV, NS), lambda b, gi, t: (0, di // NS + g + gi)),
            pl.BlockSpec((1, W), lambda b, gi, t: (0, gi)),
            pl.BlockSpec((1, NS), lambda b, gi, t: (0, di // NS + gi)),
            pl.BlockSpec((1, NS), lambda b, gi, t: (0, di // NS + g + gi)),
            pl.BlockSpec((None, 1, HG), lambda b, gi, t: (gi, 0, 0)),
            pl.BlockSpec((None, HG, 1), lambda b, gi, t: (gi, 0, 0)),
            pl.BlockSpec((None, 1, HG), lambda b, gi, t: (gi, 0, 0)),
            pl.BlockSpec((None, HG, 1), lambda b, gi, t: (gi, 0, 0)),
            pl.BlockSpec((1, W), lambda b, gi, t: (0, gi)),
            pl.BlockSpec((1, W), lambda b, gi, t: (0, gi)),
        ],
        out_specs=pl.BlockSpec((tb, W), lambda b, gi, t: (rowblk(b, gi, t), gi)),
        out_shape=jax.ShapeDtypeStruct((n, di), BF16),
        scratch_shapes=[
            pltpu.VMEM((NS, W), F32),
            pltpu.VMEM((8 + SSD_CHUNK, W + 2 * NS), F32),
        ],
        compiler_params=_cparams(("parallel", "parallel", "arbitrary")),
        name="ssd_scan",
    )(proj, proj, proj, proj, dt_g, dt_gt, conv_w, conv_w, conv_w, cb, cb, cb,
      bias, biast, a_row, a_col, dsk, norm_w.reshape(1, di))


def _gla_tables():
    c = GLA_CHUNK
    r = np.arange(c)
    d = np.zeros((GLA_LEVELS + 2, c, c), np.float32)
    masks = np.zeros((GLA_LEVELS + 1, c, c), np.float32)
    upper = np.zeros((GLA_LEVELS, c, 1), np.float32)
    masks[0] = np.eye(c)
    for j in range(1, GLA_LEVELS + 1):
        half = 1 << (j - 1)
        blk = r >> j
        mid = (blk << j) + half
        up = (r & ((1 << j) - 1)) >= half
        upper[j - 1, :, 0] = up
        for l in range(c):
            if up[l]:
                d[j - 1, l, mid[l]:l + 1] = 1.0
            else:
                d[j - 1, l, l + 1:mid[l]] = 1.0
        same = blk[:, None] == blk[None, :]
        masks[j] = same & up[:, None] & (~up)[None, :]
    d[GLA_LEVELS] = np.tril(np.ones((c, c)))
    d[GLA_LEVELS + 1] = np.triu(np.ones((c, c)), 1)
    return d.reshape(-1, c), masks, upper


def _gla_kernel(q_ref, k_ref, v_ref, g_ref, alr_ref, wa_ref, ba_ref, nw_ref,
                dall_ref, mask_ref, up_ref, o_ref, state_ref, *, n_chunks, hk):
    C = GLA_CHUNK
    scale = hk ** -0.5

    @pl.when(pl.program_id(2) == 0)
    def _():
        state_ref[...] = jnp.zeros_like(state_ref)

    wa = wa_ref[...]
    dall = dall_ref[...]

    def chunk(ci, carry):
        r0 = pl.multiple_of(ci * C, C)
        rows = pl.ds(r0, C)
        a_hi, a_lo = _split_hi_lo(alr_ref[rows, :])
        s = _dot(a_hi, wa)
        pre = s[:, :hk] + s[:, hk:] + _dot(a_lo, wa[:, :hk]) + ba_ref[...]
        lg = (jnp.minimum(pre, 0.0) - jnp.log1p(jnp.exp(-jnp.abs(pre)))) * (1.0 / GLA_GATE_TAU)
        lg_hi, lg_lo = _split_hi_lo(lg)
        e2 = _dot(dall, jnp.concatenate([lg_hi, lg_lo], axis=1))
        f = jnp.exp(e2[:, :hk] + e2[:, hk:])

        q = q_ref[rows, :].astype(F32) * scale
        k = k_ref[rows, :].astype(F32)
        v = v_ref[rows, :]
        sc = _dot_tb(q.astype(BF16), k.astype(BF16)) * mask_ref[0]
        for j in range(GLA_LEVELS):
            fj = f[j * C:(j + 1) * C, :]
            up = up_ref[j] > 0.5
            qj = jnp.where(up, q * fj, 0.0).astype(BF16)
            kj = jnp.where(up, 0.0, k * fj).astype(BF16)
            sc = sc + _dot_tb(qj, kj) * mask_ref[j + 1]
        f_cum = f[GLA_LEVELS * C:(GLA_LEVELS + 1) * C, :]
        f_end = f[(GLA_LEVELS + 1) * C:, :]
        state = state_ref[...]
        o = _dot(sc.astype(BF16), v) + _dot_tb((q * f_cum).astype(BF16), state.astype(BF16))
        state_ref[...] = state * f_cum[C - 1:C, :] + _dot_ta(v, (k * f_end).astype(BF16))

        on = o * lax.rsqrt(jnp.mean(o * o, axis=-1, keepdims=True) + EPS) * nw_ref[...]
        o_ref[rows, :] = (on * _silu(g_ref[rows, :].astype(F32))).astype(o_ref.dtype)
        return carry

    lax.fori_loop(0, n_chunks, chunk, 0)


def gla_scan(proj, alr, w_a2, b_a, norm_w, bsz, tlen, tb):
    n = bsz * tlen
    dk = w_a2.shape[1]
    hk = dk // GLA_N_HEADS
    dv = (proj.shape[1] - 2 * dk) // 2
    hv = dv // GLA_N_HEADS
    nt = tlen // tb
    H = GLA_N_HEADS
    wa = jnp.zeros((LANES, dk), F32).at[:w_a2.shape[0], :].set(w_a2)
    wa_hi, wa_lo = _split_hi_lo(wa)
    wa_cat = jnp.concatenate([wa_hi.reshape(LANES, H, hk), wa_lo.reshape(LANES, H, hk)], axis=2)
    wa_cat = wa_cat.reshape(LANES, 2 * dk)
    d_all, masks, upper = _gla_tables()
    d_all = jnp.asarray(d_all, BF16)
    masks = jnp.asarray(masks, F32)
    upper = jnp.asarray(upper, F32)
    rowblk = lambda b, h, t: b * nt + t
    kern = functools.partial(_gla_kernel, n_chunks=tb // GLA_CHUNK, hk=hk)
    return pl.pallas_call(
        kern,
        grid=(bsz, H, nt),
        in_specs=[
            pl.BlockSpec((tb, hk), lambda b, h, t: (rowblk(b, h, t), h)),
            pl.BlockSpec((tb, hk), lambda b, h, t: (rowblk(b, h, t), H + h)),
            pl.BlockSpec((tb, hv), lambda b, h, t: (rowblk(b, h, t), 2 * dk // hv + h)),
            pl.BlockSpec((tb, hv), lambda b, h, t: (rowblk(b, h, t), 2 * dk // hv + H + h)),
            pl.BlockSpec((tb, LANES), lambda b, h, t: (rowblk(b, h, t), 0)),
            pl.BlockSpec((LANES, 2 * hk), lambda b, h, t: (0, h)),
            pl.BlockSpec((1, hk), lambda b, h, t: (0, h)),
            pl.BlockSpec((1, hv), lambda b, h, t: (0, 0)),
            pl.BlockSpec(d_all.shape, lambda b, h, t: (0, 0)),
            pl.BlockSpec(masks.shape, lambda b, h, t: (0, 0, 0)),
            pl.BlockSpec(upper.shape, lambda b, h, t: (0, 0, 0)),
        ],
        out_specs=pl.BlockSpec((tb, hv), lambda b, h, t: (rowblk(b, h, t), h)),
        out_shape=jax.ShapeDtypeStruct((n, dv), BF16),
        scratch_shapes=[pltpu.VMEM((hv, hk), F32)],
        compiler_params=_cparams(("parallel", "parallel", "arbitrary")),
        name="gla_scan",
    )(proj, proj, proj, proj, alr, wa_cat, b_a.reshape(1, dk), norm_w.reshape(1, hv),
      d_all, masks, upper)


def _route_kernel(lg_ref, bias_ref, ids_ref, gates_ref, cnt_ref, base_ref):
    tm = lg_ref.shape[0]
    G, EPG, E = MOE_GROUPS, MOE_EXPERTS_PER_GROUP, MOE_N_EXPERTS

    @pl.when(pl.program_id(0) == 0)
    def _():
        base_ref[...] = jnp.zeros_like(base_ref)

    lane = lax.broadcasted_iota(I32, (tm, LANES), 1)
    logits = lg_ref[...] + bias_ref[...]
    neg = jnp.float32(-jnp.inf)
    big = jnp.int32(LANES)
    is_g = lane < G
    lg_g = jnp.where(is_g, logits, neg)
    m_g = jnp.max(lg_g, axis=-1, keepdims=True)
    gi = jnp.min(jnp.where(lg_g == m_g, lane, big), axis=-1, keepdims=True)
    z_g = jnp.sum(jnp.where(is_g, jnp.exp(lg_g - m_g), 0.0), axis=-1, keepdims=True)
    grp_w = 1.0 / z_g
    in_grp = (lane >= G) & (lane < G + E) & (((lane - G) // EPG) == gi)
    lg_e = jnp.where(in_grp, logits, neg)
    m1 = jnp.max(lg_e, axis=-1, keepdims=True)
    l1 = jnp.min(jnp.where(lg_e == m1, lane, big), axis=-1, keepdims=True)
    lg_e2 = jnp.where(lane == l1, neg, lg_e)
    m2 = jnp.max(lg_e2, axis=-1, keepdims=True)
    l2 = jnp.min(jnp.where(lg_e2 == m2, lane, big), axis=-1, keepdims=True)
    p2 = jnp.exp(m2 - m1)
    g1 = grp_w / (1.0 + p2)
    g2 = grp_w * p2 / (1.0 + p2)
    e1 = l1 - G
    e2 = l2 - G
    onehot = jnp.where((lane == e1) | (lane == e2), 1.0, 0.0)
    r = lax.broadcasted_iota(I32, (tm, tm), 0)
    c = lax.broadcasted_iota(I32, (tm, tm), 1)
    strict = jnp.where(c < r, 1.0, 0.0).astype(BF16)
    cum = _dot(strict, onehot.astype(BF16)) + base_ref[0:1, :]
    rank1 = jnp.sum(jnp.where(lane == e1, cum, 0.0), axis=-1, keepdims=True).astype(I32)
    rank2 = jnp.sum(jnp.where(lane == e2, cum, 0.0), axis=-1, keepdims=True).astype(I32)
    total = base_ref[0:1, :] + jnp.sum(onehot, axis=0, keepdims=True)
    base_ref[...] = jnp.broadcast_to(total, base_ref.shape)
    cnt_ref[...] = jnp.broadcast_to(total, cnt_ref.shape).astype(I32)

    ids = jnp.where(lane == 0, e1, jnp.where(lane == 1, e2, jnp.where(lane == 2, rank1,
                    jnp.where(lane == 3, rank2, 0))))
    ids_ref[...] = ids
    gates_ref[...] = jnp.where(lane == 0, g1, jnp.where(lane == 1, g2, 0.0))


def moe_route(logits, bias, tm):
    n = logits.shape[0]
    return pl.pallas_call(
        _route_kernel,
        grid=(n // tm,),
        in_specs=[
            pl.BlockSpec((tm, LANES), lambda i: (i, 0)),
            pl.BlockSpec((1, LANES), lambda i: (0, 0)),
        ],
        out_specs=[
            pl.BlockSpec((tm, LANES), lambda i: (i, 0)),
            pl.BlockSpec((tm, LANES), lambda i: (i, 0)),
            pl.BlockSpec((8, LANES), lambda i: (0, 0)),
        ],
        out_shape=[
            jax.ShapeDtypeStruct((n, LANES), I32),
            jax.ShapeDtypeStruct((n, LANES), F32),
            jax.ShapeDtypeStruct((8, LANES), I32),
        ],
        scratch_shapes=[pltpu.VMEM((8, LANES), F32)],
        compiler_params=_cparams(("arbitrary",)),
        name="moe_route",
    )(logits, bias)


def _dispatch_kernel(dest_ref, h_ref, xz_ref, xb_ref, sem):
    del xz_ref
    tm = h_ref.shape[0]

    def issue(r, carry):
        for s in range(MOE_TOP_K):
            d = dest_ref[0, 0, MOE_TOP_K * r + s]
            pltpu.make_async_copy(h_ref.at[pl.ds(r, 1)], xb_ref.at[pl.ds(d, 1)], sem).start()
        return carry

    lax.fori_loop(0, tm, issue, 0)

    def drain(r, carry):
        for s in range(MOE_TOP_K):
            pltpu.make_async_copy(h_ref.at[pl.ds(0, 1)], xb_ref.at[pl.ds(0, 1)], sem).wait()
        return carry

    lax.fori_loop(0, tm, drain, 0)


def moe_dispatch(hp, dest, n_rows, tm):
    n, dh = hp.shape
    dest3 = dest.reshape(n // tm, 1, MOE_TOP_K * tm)
    xz = jnp.zeros((n_rows, dh), U32)
    return pl.pallas_call(
        _dispatch_kernel,
        grid=(n // tm,),
        in_specs=[
            pl.BlockSpec((1, 1, MOE_TOP_K * tm), lambda i: (i, 0, 0), memory_space=pltpu.SMEM),
            pl.BlockSpec((tm, dh), lambda i: (i, 0)),
            pl.BlockSpec(memory_space=pl.ANY),
        ],
        out_specs=pl.BlockSpec(memory_space=pl.ANY),
        out_shape=jax.ShapeDtypeStruct((n_rows, dh), U32),
        scratch_shapes=[pltpu.SemaphoreType.DMA(())],
        input_output_aliases={2: 0},
        compiler_params=_cparams(("arbitrary",)),
        name="moe_dispatch",
    )(dest3, hp, xz)


def _experts_kernel(be_ref, nb_ref, x_ref, wgu_ref, wd_ref, y_ref):
    i = pl.program_id(0)

    @pl.when(i < nb_ref[0])
    def _():
        xa, xb = _unpack_bf16_pair(x_ref[...])
        half = xa.shape[1]
        h = _dot(xa.astype(BF16), wgu_ref[:half, :]) + _dot(xb.astype(BF16), wgu_ref[half:, :])
        ff = h.shape[1] // 2
        act = (_silu(h[:, :ff]) * h[:, ff:]).astype(BF16)
        y = _dot(act, wd_ref[...])
        y_ref[...] = _pack_bf16_pair(y[:, :half], y[:, half:])

    @pl.when(i >= nb_ref[0])
    def _():
        y_ref[...] = jnp.zeros_like(y_ref)


def moe_experts(xbuf, block_e, n_used, w_gu, w_d):
    n_rows, dh = xbuf.shape
    nb = n_rows // MOE_ROWS
    d, ff2 = w_gu.shape[1], w_gu.shape[2]
    grid_spec = pltpu.PrefetchScalarGridSpec(
        num_scalar_prefetch=2,
        grid=(nb,),
        in_specs=[
            pl.BlockSpec((MOE_ROWS, dh), lambda i, be, nu: (i, 0)),
            pl.BlockSpec((None, d, ff2), lambda i, be, nu: (be[i], 0, 0)),
            pl.BlockSpec((None, ff2 // 2, d), lambda i, be, nu: (be[i], 0, 0)),
        ],
        out_specs=pl.BlockSpec((MOE_ROWS, dh), lambda i, be, nu: (i, 0)),
    )
    return pl.pallas_call(
        _experts_kernel,
        grid_spec=grid_spec,
        out_shape=jax.ShapeDtypeStruct((n_rows, dh), U32),
        compiler_params=_cparams(("arbitrary",)),
        name="moe_experts",
    )(block_e, n_used, xbuf, w_gu, w_d)


def _combine_kernel(dest_ref, x_ref, gates_ref, nw_ref, yb_ref, o_ref, buf_ref, sem, *, final_norm):
    tm = x_ref.shape[0]

    def issue(r, carry):
        for s in range(MOE_TOP_K):
            d = dest_ref[0, 0, MOE_TOP_K * r + s]
            pltpu.make_async_copy(yb_ref.at[pl.ds(d, 1)], buf_ref.at[s, pl.ds(r, 1)], sem).start()
        return carry

    lax.fori_loop(0, tm, issue, 0)

    def drain(r, carry):
        for s in range(MOE_TOP_K):
            pltpu.make_async_copy(yb_ref.at[pl.ds(0, 1)], buf_ref.at[0, pl.ds(0, 1)], sem).wait()
        return carry

    lax.fori_loop(0, tm, drain, 0)

    gates = gates_ref[...]
    a1, b1 = _unpack_bf16_pair(buf_ref[0])
    a2, b2 = _unpack_bf16_pair(buf_ref[1])
    g1 = gates[:, 0:1]
    g2 = gates[:, 1:2]
    x = x_ref[...]
    half = a1.shape[1]
    xn = jnp.concatenate([x[:, :half] + g1 * a1 + g2 * a2, x[:, half:] + g1 * b1 + g2 * b2], axis=1)
    if final_norm:
        xn = xn * lax.rsqrt(jnp.mean(xn * xn, axis=-1, keepdims=True) + EPS) * nw_ref[...]
    o_ref[...] = xn


def moe_combine(x, ybuf, dest, gates, nw, tm, final_norm):
    n, d = x.shape
    dh = d // 2
    dest3 = dest.reshape(n // tm, 1, MOE_TOP_K * tm)
    kern = functools.partial(_combine_kernel, final_norm=final_norm)
    return pl.pallas_call(
        kern,
        grid=(n // tm,),
        in_specs=[
            pl.BlockSpec((1, 1, MOE_TOP_K * tm), lambda i: (i, 0, 0), memory_space=pltpu.SMEM),
            pl.BlockSpec((tm, d), lambda i: (i, 0)),
            pl.BlockSpec((tm, LANES), lambda i: (i, 0)),
            pl.BlockSpec((1, d), lambda i: (0, 0)),
            pl.BlockSpec(memory_space=pl.ANY),
        ],
        out_specs=pl.BlockSpec((tm, d), lambda i: (i, 0)),
        out_shape=jax.ShapeDtypeStruct((n, d), F32),
        scratch_shapes=[pltpu.VMEM((MOE_TOP_K, tm, dh), U32), pltpu.SemaphoreType.DMA(())],
        compiler_params=_cparams(("arbitrary",)),
        name="moe_combine",
    )(dest3, x, gates, nw.reshape(1, d), ybuf)


def hier_moe(x, hp, logits, b_group, b_expert, w_gu, w_d, nw_final, final_norm, tiles):
    n = x.shape[0]
    E = MOE_N_EXPERTS
    bias = jnp.zeros((1, LANES), F32).at[0, :MOE_GROUPS].set(b_group).at[0, MOE_GROUPS:MOE_GROUPS + E].set(b_expert)
    ids, gates, cnt = moe_route(logits, bias, tiles["route"])
    counts = cnt[0, :E]
    pcounts = (counts + MOE_ROWS - 1) // MOE_ROWS * MOE_ROWS
    pends = jnp.cumsum(pcounts)
    poffs = pends - pcounts
    n_blocks = (n * MOE_TOP_K) // MOE_ROWS + E
    dest = poffs[ids[:, 0:2]] + ids[:, 2:4]
    blk_start = jnp.arange(n_blocks, dtype=I32) * MOE_ROWS
    n_used = (pends[-1] // MOE_ROWS).astype(I32)
    block_e = jnp.minimum(jnp.searchsorted(pends, blk_start, side="right"), E - 1).astype(I32)
    last_e = block_e[jnp.maximum(n_used - 1, 0)]
    block_e = jnp.where(jnp.arange(n_blocks) < n_used, block_e, last_e)
    xbuf = moe_dispatch(hp, dest, n_blocks * MOE_ROWS, tiles["dispatch"])
    ybuf = moe_experts(xbuf, block_e, n_used.reshape(1), w_gu, w_d)
    return moe_combine(x, ybuf, dest, gates, nw_final, tiles["combine"], final_norm)


def _tiles(n, tlen):
    pick = lambda pref, total: next(t for t in (pref, 512, 256, 128, 64) if t <= pref and total % t == 0)
    return {
        "mm_m": pick(1024, n), "mm_n": 1024, "out_m": pick(256, n),
        "ssd_t": pick(512, tlen), "gla_t": pick(512, tlen),
        "route": pick(512, n), "dispatch": pick(256, n), "combine": pick(256, n),
    }


def kernel(x, ssd_w_in, ssd_conv_w, ssd_conv_b, ssd_dt_bias, ssd_a_log, ssd_d, ssd_norm_w, ssd_w_out,
           gla_w_in, gla_w_a2, gla_b_a, gla_norm_w, gla_w_out, norm_mix, norm_ffn, moe_w_group,
           moe_b_group, moe_w_expert, moe_b_expert, moe_w_gate, moe_w_up, moe_w_down, norm_final):
    bsz, tlen, d = x.shape
    n = bsz * tlen
    depth = norm_mix.shape[0]
    tiles = _tiles(n, tlen)
    xf = x.reshape(n, d)
    for i in range(depth):
        j = i // 2
        if i % 2 == 0:
            n_heads = ssd_dt_bias.shape[1]
            w_in = ssd_w_in[j]
            m_main = w_in.shape[1] - n_heads
            proj, dtraw = norm_matmul(xf, norm_mix[i], w_in[:, :m_main].astype(BF16),
                                      _small_weight(w_in[:, m_main:]), tiles["mm_m"], tiles["mm_n"])
            y = ssd_scan(proj, dtraw, ssd_conv_w[j], ssd_conv_b[j], ssd_dt_bias[j], ssd_a_log[j],
                         ssd_d[j], ssd_norm_w[j], bsz, tlen, tiles["ssd_t"])
            w_out = ssd_w_out[j]
        else:
            w_in = gla_w_in[j]
            m_main = w_in.shape[1] - GLA_GATE_RANK
            proj, alr = norm_matmul(xf, norm_mix[i], w_in[:, :m_main].astype(BF16),
                                    _small_weight(w_in[:, m_main:]), tiles["mm_m"], tiles["mm_n"])
            y = gla_scan(proj, alr, gla_w_a2[j], gla_b_a[j], gla_norm_w[j], bsz, tlen, tiles["gla_t"])
            w_out = gla_w_out[j]
        w_router = _small_weight(jnp.concatenate([moe_w_group[i], moe_w_expert[i]], axis=1))
        xf, hp, logits = out_proj(y, w_out.astype(BF16), xf, norm_ffn[i], w_router, tiles["out_m"])
        w_gu = jnp.concatenate([moe_w_gate[i], moe_w_up[i]], axis=2).astype(BF16)
        w_d = moe_w_down[i].astype(BF16)
        xf = hier_moe(xf, hp, logits, moe_b_group[i], moe_b_expert[i], w_gu, w_d, norm_final,
                      i == depth - 1, tiles)
    return xf.reshape(bsz, tlen, d)
```

```python
import functools

import numpy as np
import jax
import jax.numpy as jnp
from jax import lax
from jax.experimental import pallas as pl
from jax.experimental.pallas import tpu as pltpu

F32 = jnp.float32
BF16 = jnp.bfloat16
U32 = jnp.uint32
I32 = jnp.int32

EPS = 1e-6
LANES = 128
BF16_ROWS = 16
VMEM_LIMIT = 56 * 1024 * 1024

SSD_HEAD_DIM = 64
SSD_HEADS_PER_GROUP = 8
SSD_D_STATE = 128
SSD_CONV = 4
SSD_CHUNK = 128
SSD_GROUP_W = SSD_HEAD_DIM * SSD_HEADS_PER_GROUP

GLA_N_HEADS = 4
GLA_GATE_RANK = 16
GLA_GATE_TAU = 16.0
GLA_CHUNK = 64
GLA_LEVELS = 6

MOE_GROUPS = 8
MOE_EXPERTS_PER_GROUP = 8
MOE_N_EXPERTS = 64
MOE_TOP_K = 2
MOE_ROWS = 256
DMA_UNROLL = 8


def _cparams(sem):
    return pltpu.CompilerParams(dimension_semantics=sem, vmem_limit_bytes=VMEM_LIMIT)


def _split_hi_lo(v):
    hi = v.astype(BF16)
    lo = (v - hi.astype(F32)).astype(BF16)
    return hi, lo


def _dot(a, b):
    return jnp.dot(a, b, preferred_element_type=F32)


def _dot_tb(a, b):
    return lax.dot_general(a, b, (((1,), (1,)), ((), ())), preferred_element_type=F32)


def _dot_ta(a, b):
    return lax.dot_general(a, b, (((0,), (0,)), ((), ())), preferred_element_type=F32)


def _silu(v):
    return 0.5 * v * (1.0 + jnp.tanh(0.5 * v))


def _pack_bf16_pair(a, b):
    ua = lax.bitcast_convert_type(a.astype(BF16).astype(F32), U32)
    ub = lax.bitcast_convert_type(b.astype(BF16).astype(F32), U32)
    return ua | (ub >> 16)


def _unpack_bf16_pair(u):
    a = lax.bitcast_convert_type(u & jnp.uint32(0xFFFF0000), F32)
    b = lax.bitcast_convert_type(u << 16, F32)
    return a, b


def _norm_matmul_kernel(x_ref, nw_ref, w_ref, ws_ref, o_ref, os_ref, h_ref):
    j = pl.program_id(1)

    @pl.when(j == 0)
    def _():
        x = x_ref[...]
        h = x * lax.rsqrt(jnp.mean(x * x, axis=-1, keepdims=True) + EPS) * nw_ref[...]
        hi, lo = _split_hi_lo(h)
        h_ref[...] = hi
        s = _dot(hi, ws_ref[...])
        os_ref[...] = s[:, :LANES] + s[:, LANES:] + _dot(lo, ws_ref[:, :LANES])

    o_ref[...] = _dot(h_ref[...], w_ref[...]).astype(o_ref.dtype)


def _small_weight(w):
    k, n = w.shape
    wp = jnp.zeros((k, LANES), F32).at[:, :n].set(w)
    hi, lo = _split_hi_lo(wp)
    return jnp.concatenate([hi, lo], axis=1)


def norm_matmul(x, nw, w_main, m, w_small, tm, tn):
    n, d = x.shape
    return pl.pallas_call(
        _norm_matmul_kernel,
        grid=(n // tm, m // tn),
        in_specs=[
            pl.BlockSpec((tm, d), lambda i, j: (i, 0)),
            pl.BlockSpec((1, d), lambda i, j: (0, 0)),
            pl.BlockSpec((d, tn), lambda i, j: (0, j)),
            pl.BlockSpec((d, 2 * LANES), lambda i, j: (0, 0)),
        ],
        out_specs=[
            pl.BlockSpec((tm, tn), lambda i, j: (i, j)),
            pl.BlockSpec((tm, LANES), lambda i, j: (i, 0)),
        ],
        out_shape=[
            jax.ShapeDtypeStruct((n, m), BF16),
            jax.ShapeDtypeStruct((n, LANES), F32),
        ],
        scratch_shapes=[pltpu.VMEM((tm, d), BF16)],
        compiler_params=_cparams(("parallel", "arbitrary")),
        name="norm_matmul",
    )(x, nw.reshape(1, d), w_main, w_small)


def _out_proj_kernel(y_ref, w_ref, x_ref, nw_ref, wr_ref, xo_ref, hp_ref, lg_ref):
    xn = x_ref[...] + _dot(y_ref[...], w_ref[...])
    xo_ref[...] = xn
    h = xn * lax.rsqrt(jnp.mean(xn * xn, axis=-1, keepdims=True) + EPS) * nw_ref[...]
    half = h.shape[1] // 2
    hp_ref[...] = _pack_bf16_pair(h[:, :half], h[:, half:])
    hi, lo = _split_hi_lo(h)
    s = _dot(hi, wr_ref[...])
    lg_ref[...] = s[:, :LANES] + s[:, LANES:] + _dot(lo, wr_ref[:, :LANES])


def out_proj(y, w_out, x, nw, w_router, tm):
    n, k = y.shape
    d = x.shape[1]
    return pl.pallas_call(
        _out_proj_kernel,
        grid=(n // tm,),
        in_specs=[
            pl.BlockSpec((tm, k), lambda i: (i, 0)),
            pl.BlockSpec((k, d), lambda i: (0, 0)),
            pl.BlockSpec((tm, d), lambda i: (i, 0)),
            pl.BlockSpec((1, d), lambda i: (0, 0)),
            pl.BlockSpec((d, 2 * LANES), lambda i: (0, 0)),
        ],
        out_specs=[
            pl.BlockSpec((tm, d), lambda i: (i, 0)),
            pl.BlockSpec((tm, d // 2), lambda i: (i, 0)),
            pl.BlockSpec((tm, LANES), lambda i: (i, 0)),
        ],
        out_shape=[
            jax.ShapeDtypeStruct((n, d), F32),
            jax.ShapeDtypeStruct((n, d // 2), U32),
            jax.ShapeDtypeStruct((n, LANES), F32),
        ],
        compiler_params=_cparams(("parallel",)),
        name="out_proj",
    )(y, w_out, x, nw.reshape(1, d), w_router)


def _ssd_kernel(z_ref, xs_ref, b_ref, c_ref, dt_ref, selc_ref, selr_ref,
                cwx_ref, cwb_ref, cwc_ref, cbx_ref, cbb_ref, cbc_ref,
                bias_ref, biast_ref, a_ref, at_ref, dsk_ref, nw_ref,
                y_ref, state_ref, tail_ref, *, n_chunks):
    L = SSD_CHUNK
    W = SSD_GROUP_W
    NS = SSD_D_STATE
    HG = SSD_HEADS_PER_GROUP
    P = SSD_HEAD_DIM
    TAIL = BF16_ROWS
    WC = W + 2 * NS
    tb = n_chunks * L

    @pl.when(pl.program_id(2) == 0)
    def _():
        state_ref[...] = jnp.zeros_like(state_ref)
        tail_ref[...] = jnp.zeros_like(tail_ref)

    row = lax.broadcasted_iota(I32, (L, L), 0)
    col = lax.broadcasted_iota(I32, (L, L), 1)
    causal = row >= col
    tri = jnp.where(causal, 1.0, 0.0).astype(BF16)
    trit = jnp.where(col >= row, 1.0, 0.0).astype(BF16)
    er = lax.broadcasted_iota(I32, (2 * HG, W), 0)
    ec = lax.broadcasted_iota(I32, (2 * HG, W), 1)
    expand = jnp.where((er % HG) == (ec // P), 1.0, 0.0).astype(BF16)
    sr = lax.broadcasted_iota(I32, ((SSD_CONV - 1) * L, L + TAIL), 0)
    sc = lax.broadcasted_iota(I32, ((SSD_CONV - 1) * L, L + TAIL), 1)
    shift = jnp.where(sc == (sr % L) + (sr // L) + (TAIL - SSD_CONV + 1), 1.0, 0.0).astype(BF16)

    cw = jnp.concatenate([cwx_ref[...], cwb_ref[...], cwc_ref[...]], axis=1)
    cb = jnp.concatenate([cbx_ref[...], cbb_ref[...], cbc_ref[...]], axis=1)
    a_row = a_ref[...]
    a_col = at_ref[...]

    def expand_heads(v):
        hi, lo = _split_hi_lo(v)
        return _dot(jnp.concatenate([hi, lo], axis=1), expand)

    d = dt_ref[...]
    d1 = d.astype(BF16)
    r1 = d - d1.astype(F32)
    d2 = r1.astype(BF16)
    d3 = (r1 - d2.astype(F32)).astype(BF16)
    dcat = jnp.concatenate([d1, d2, d3], axis=0)
    c3 = _dot(dcat, selc_ref[...])
    dt_all = jax.nn.softplus(c3[:tb] + c3[tb:2 * tb] + c3[2 * tb:] + bias_ref[...])
    r3 = _dot_tb(selr_ref[...], dcat)
    dtt_all = jax.nn.softplus(r3[:, :tb] + r3[:, tb:2 * tb] + r3[:, 2 * tb:] + biast_ref[...])

    u_all = jnp.concatenate([xs_ref[...], b_ref[...], c_ref[...]], axis=1)
    ucat = jnp.concatenate([tail_ref[...], u_all], axis=0)
    tail_ref[...] = u_all[tb - TAIL:, :]

    for ci in range(n_chunks):
        r0 = ci * L
        shifted = _dot(shift, ucat[r0:r0 + L + TAIL, :])
        acc = cb + u_all[r0:r0 + L, :].astype(F32) * cw[SSD_CONV - 1:SSD_CONV, :]
        for k in range(SSD_CONV - 1):
            acc = acc + shifted[k * L:(k + 1) * L, :] * cw[k:k + 1, :]
        conv = _silu(acc)
        xc = conv[:, :W]
        bc = conv[:, W:W + NS].astype(BF16)
        cc = conv[:, W + NS:].astype(BF16)

        dt = dt_all[r0:r0 + L, :]
        da_hi, da_lo = _split_hi_lo(dt * a_row)
        acs2 = _dot(tri, jnp.concatenate([da_hi, da_lo], axis=1))
        acs = acs2[:, :HG] + acs2[:, HG:]
        dat_hi, dat_lo = _split_hi_lo(dtt_all[:, r0:r0 + L] * a_col)
        acst2 = _dot(jnp.concatenate([dat_hi, dat_lo], axis=0), trit)
        acst = acst2[:HG, :] + acst2[HG:, :]

        acs_last = acs[L - 1:L, :]
        dt_e = expand_heads(dt)
        eacs_e = expand_heads(jnp.exp(acs))
        wend_e = expand_heads(dt * jnp.exp(acs_last - acs))
        xdt_b = (xc * dt_e).astype(BF16)

        cbm = _dot_tb(cc, bc)
        state = state_ref[...]
        y_off = _dot(cc, state.astype(BF16)) * eacs_e
        ys = []
        for h in range(HG):
            seg = acs[:, h:h + 1] - acst[h:h + 1, :]
            m = jnp.where(causal, jnp.exp(seg), 0.0) * cbm
            ys.append(_dot(m.astype(BF16), xdt_b[:, h * P:(h + 1) * P]))
        y = jnp.concatenate(ys, axis=1) + y_off + xc * dsk_ref[...]
        state_ref[...] = state * eacs_e[L - 1:L, :] + _dot_ta(bc, (xc * wend_e).astype(BF16))

        yg = y * _silu(z_ref[r0:r0 + L, :].astype(F32))
        yn = yg * lax.rsqrt(jnp.mean(yg * yg, axis=-1, keepdims=True) + EPS) * nw_ref[...]
        y_ref[r0:r0 + L, :] = yn.astype(y_ref.dtype)


def ssd_scan(proj, dtraw, conv_w, conv_b, dt_bias, a_log, d_skip, norm_w, bsz, tlen, tb):
    n = bsz * tlen
    n_heads = dt_bias.shape[0]
    g = n_heads // SSD_HEADS_PER_GROUP
    di = g * SSD_GROUP_W
    W, NS, HG = SSD_GROUP_W, SSD_D_STATE, SSD_HEADS_PER_GROUP
    nt = tlen // tb
    lane = np.arange(LANES)
    sel = (lane[None, :, None] == (np.arange(g)[:, None, None] * HG + np.arange(HG)[None, None, :]))
    selc = jnp.asarray(sel, BF16)
    selr = jnp.asarray(sel.transpose(0, 2, 1), BF16)
    bias = dt_bias.reshape(g, 1, HG)
    biast = dt_bias.reshape(g, HG, 1)
    a = -jnp.exp(a_log.astype(F32))
    a_row = a.reshape(g, 1, HG)
    a_col = a.reshape(g, HG, 1)
    dsk = jnp.repeat(d_skip.astype(F32), SSD_HEAD_DIM).reshape(1, di)
    cb = conv_b.reshape(1, -1)
    xoff = di // W
    boff = 2 * di // NS
    coff = boff + g
    rowblk = lambda b, gi, t: b * nt + t
    kern = functools.partial(_ssd_kernel, n_chunks=tb // SSD_CHUNK)
    return pl.pallas_call(
        kern,
        grid=(bsz, g, nt),
        in_specs=[
            pl.BlockSpec((tb, W), lambda b, gi, t: (rowblk(b, gi, t), gi)),
            pl.BlockSpec((tb, W), lambda b, gi, t: (rowblk(b, gi, t), xoff + gi)),
            pl.BlockSpec((tb, NS), lambda b, gi, t: (rowblk(b, gi, t), boff + gi)),
            pl.BlockSpec((tb, NS), lambda b, gi, t: (rowblk(b, gi, t), coff + gi)),
            pl.BlockSpec((tb, LANES), lambda b, gi, t: (rowblk(b, gi, t), 0)),
            pl.BlockSpec((None, LANES, HG), lambda b, gi, t: (gi, 0, 0)),
            pl.BlockSpec((None, HG, LANES), lambda b, gi, t: (gi, 0, 0)),
            pl.BlockSpec((SSD_CONV, W), lambda b, gi, t: (0, gi)),
            pl.BlockSpec((SSD_CONV, NS), lambda b, gi, t: (0, di // NS + gi)),
            pl.BlockSpec((SSD_CONV, NS), lambda b, gi, t: (0, di // NS + g + gi)),
            pl.BlockSpec((1, W), lambda b, gi, t: (0, gi)),
            pl.BlockSpec((1, NS), lambda b, gi, t: (0, di // NS + gi)),
            pl.BlockSpec((1, NS), lambda b, gi, t: (0, di // NS + g + gi)),
            pl.BlockSpec((None, 1, HG), lambda b, gi, t: (gi, 0, 0)),
            pl.BlockSpec((None, HG, 1), lambda b, gi, t: (gi, 0, 0)),
            pl.BlockSpec((None, 1, HG), lambda b, gi, t: (gi, 0, 0)),
            pl.BlockSpec((None, HG, 1), lambda b, gi, t: (gi, 0, 0)),
            pl.BlockSpec((1, W), lambda b, gi, t: (0, gi)),
            pl.BlockSpec((1, W), lambda b, gi, t: (0, gi)),
        ],
        out_specs=pl.BlockSpec((tb, W), lambda b, gi, t: (rowblk(b, gi, t), gi)),
        out_shape=jax.ShapeDtypeStruct((n, di), BF16),
        scratch_shapes=[
            pltpu.VMEM((NS, W), F32),
            pltpu.VMEM((BF16_ROWS, W + 2 * NS), BF16),
        ],
        compiler_params=_cparams(("parallel", "parallel", "arbitrary")),
        name="ssd_scan",
    )(proj, proj, proj, proj, dtraw, selc, selr, conv_w, conv_w, conv_w, cb, cb, cb,
      bias, biast, a_row, a_col, dsk, norm_w.reshape(1, di))


def _gla_tables():
    c = GLA_CHUNK
    r = np.arange(c)
    d = np.zeros((GLA_LEVELS + 2, c, c), np.float32)
    masks = np.zeros((GLA_LEVELS + 1, c, c), np.float32)
    upper = np.zeros((GLA_LEVELS, c, 1), np.float32)
    masks[0] = np.eye(c)
    for j in range(1, GLA_LEVELS + 1):
        half = 1 << (j - 1)
        blk = r >> j
        mid = (blk << j) + half
        up = (r & ((1 << j) - 1)) >= half
        upper[j - 1, :, 0] = up
        for l in range(c):
            if up[l]:
                d[j - 1, l, mid[l]:l + 1] = 1.0
            else:
                d[j - 1, l, l + 1:mid[l]] = 1.0
        same = blk[:, None] == blk[None, :]
        masks[j] = same & up[:, None] & (~up)[None, :]
    d[GLA_LEVELS] = np.tril(np.ones((c, c)))
    d[GLA_LEVELS + 1] = np.triu(np.ones((c, c)), 1)
    return d.reshape(-1, c), masks, upper


def _gla_kernel(q_ref, k_ref, v_ref, g_ref, alr_ref, wa_ref, ba_ref, nw_ref,
                dall_ref, mask_ref, up_ref, o_ref, state_ref, *, n_chunks, hk):
    C = GLA_CHUNK
    scale = hk ** -0.5

    @pl.when(pl.program_id(2) == 0)
    def _():
        state_ref[...] = jnp.zeros_like(state_ref)

    wa = wa_ref[...]
    dall = dall_ref[...]

    a_hi, a_lo = _split_hi_lo(alr_ref[...])
    s = _dot(a_hi, wa)
    pre = s[:, :hk] + s[:, hk:] + _dot(a_lo, wa[:, :hk]) + ba_ref[...]
    lg = (jnp.minimum(pre, 0.0) - jnp.log1p(jnp.exp(-jnp.abs(pre)))) * (1.0 / GLA_GATE_TAU)
    lg_hi, lg_lo = _split_hi_lo(lg)
    lg2 = jnp.concatenate([lg_hi, lg_lo], axis=1)

    for ci in range(n_chunks):
        r0 = ci * C
        e2 = _dot(dall, lg2[r0:r0 + C, :])
        f = jnp.exp(e2[:, :hk] + e2[:, hk:])

        q = q_ref[r0:r0 + C, :].astype(F32) * scale
        k = k_ref[r0:r0 + C, :].astype(F32)
        v = v_ref[r0:r0 + C, :]
        sc = _dot_tb(q.astype(BF16), k.astype(BF16)) * mask_ref[0]
        for j in range(GLA_LEVELS):
            fj = f[j * C:(j + 1) * C, :]
            up = up_ref[j] > 0.5
            qj = jnp.where(up, q * fj, 0.0).astype(BF16)
            kj = jnp.where(up, 0.0, k * fj).astype(BF16)
            sc = sc + _dot_tb(qj, kj) * mask_ref[j + 1]
        f_cum = f[GLA_LEVELS * C:(GLA_LEVELS + 1) * C, :]
        f_end = f[(GLA_LEVELS + 1) * C:, :]
        state = state_ref[...]
        o = _dot(sc.astype(BF16), v) + _dot_tb((q * f_cum).astype(BF16), state.astype(BF16))
        state_ref[...] = state * f_cum[C - 1:C, :] + _dot_ta(v, (k * f_end).astype(BF16))

        on = o * lax.rsqrt(jnp.mean(o * o, axis=-1, keepdims=True) + EPS) * nw_ref[...]
        o_ref[r0:r0 + C, :] = (on * _silu(g_ref[r0:r0 + C, :].astype(F32))).astype(o_ref.dtype)


def gla_scan(proj, alr, w_a2, b_a, norm_w, bsz, tlen, tb):
    n = bsz * tlen
    dk = w_a2.shape[1]
    hk = dk // GLA_N_HEADS
    dv = (proj.shape[1] - 2 * dk) // 2
    hv = dv // GLA_N_HEADS
    nt = tlen // tb
    H = GLA_N_HEADS
    wa = jnp.zeros((LANES, dk), F32).at[:w_a2.shape[0], :].set(w_a2)
    wa_hi, wa_lo = _split_hi_lo(wa)
    wa_cat = jnp.concatenate([wa_hi.reshape(LANES, H, hk), wa_lo.reshape(LANES, H, hk)], axis=2)
    wa_cat = wa_cat.reshape(LANES, 2 * dk)
    d_all, masks, upper = _gla_tables()
    d_all = jnp.asarray(d_all, BF16)
    masks = jnp.asarray(masks, F32)
    upper = jnp.asarray(upper, F32)
    rowblk = lambda b, h, t: b * nt + t
    kern = functools.partial(_gla_kernel, n_chunks=tb // GLA_CHUNK, hk=hk)
    return pl.pallas_call(
        kern,
        grid=(bsz, H, nt),
        in_specs=[
            pl.BlockSpec((tb, hk), lambda b, h, t: (rowblk(b, h, t), h)),
            pl.BlockSpec((tb, hk), lambda b, h, t: (rowblk(b, h, t), H + h)),
            pl.BlockSpec((tb, hv), lambda b, h, t: (rowblk(b, h, t), 2 * dk // hv + h)),
            pl.BlockSpec((tb, hv), lambda b, h, t: (rowblk(b, h, t), 2 * dk // hv + H + h)),
            pl.BlockSpec((tb, LANES), lambda b, h, t: (rowblk(b, h, t), 0)),
            pl.BlockSpec((LANES, 2 * hk), lambda b, h, t: (0, h)),
            pl.BlockSpec((1, hk), lambda b, h, t: (0, h)),
            pl.BlockSpec((1, hv), lambda b, h, t: (0, 0)),
            pl.BlockSpec(d_all.shape, lambda b, h, t: (0, 0)),
            pl.BlockSpec(masks.shape, lambda b, h, t: (0, 0, 0)),
            pl.BlockSpec(upper.shape, lambda b, h, t: (0, 0, 0)),
        ],
        out_specs=pl.BlockSpec((tb, hv), lambda b, h, t: (rowblk(b, h, t), h)),
        out_shape=jax.ShapeDtypeStruct((n, dv), BF16),
        scratch_shapes=[pltpu.VMEM((hv, hk), F32)],
        compiler_params=_cparams(("parallel", "parallel", "arbitrary")),
        name="gla_scan",
    )(proj, proj, proj, proj, alr, wa_cat, b_a.reshape(1, dk), norm_w.reshape(1, hv),
      d_all, masks, upper)


def _route_kernel(lg_ref, bias_ref, ids_ref, gates_ref, cnt_ref, base_ref):
    tm = lg_ref.shape[0]
    G, EPG, E = MOE_GROUPS, MOE_EXPERTS_PER_GROUP, MOE_N_EXPERTS

    @pl.when(pl.program_id(0) == 0)
    def _():
        base_ref[...] = jnp.zeros_like(base_ref)

    lane = lax.broadcasted_iota(I32, (tm, LANES), 1)
    logits = lg_ref[...] + bias_ref[...]
    neg = jnp.float32(-jnp.inf)
    big = jnp.int32(LANES)
    is_g = lane < G
    lg_g = jnp.where(is_g, logits, neg)
    m_g = jnp.max(lg_g, axis=-1, keepdims=True)
    gi = jnp.min(jnp.where(lg_g == m_g, lane, big), axis=-1, keepdims=True)
    z_g = jnp.sum(jnp.where(is_g, jnp.exp(lg_g - m_g), 0.0), axis=-1, keepdims=True)
    grp_w = 1.0 / z_g
    in_grp = (lane >= G) & (lane < G + E) & (((lane - G) // EPG) == gi)
    lg_e = jnp.where(in_grp, logits, neg)
    m1 = jnp.max(lg_e, axis=-1, keepdims=True)
    l1 = jnp.min(jnp.where(lg_e == m1, lane, big), axis=-1, keepdims=True)
    lg_e2 = jnp.where(lane == l1, neg, lg_e)
    m2 = jnp.max(lg_e2, axis=-1, keepdims=True)
    l2 = jnp.min(jnp.where(lg_e2 == m2, lane, big), axis=-1, keepdims=True)
    p2 = jnp.exp(m2 - m1)
    g1 = grp_w / (1.0 + p2)
    g2 = grp_w * p2 / (1.0 + p2)
    e1 = l1 - G
    e2 = l2 - G
    onehot = jnp.where((lane == e1) | (lane == e2), 1.0, 0.0)
    r = lax.broadcasted_iota(I32, (tm, tm), 0)
    c = lax.broadcasted_iota(I32, (tm, tm), 1)
    strict = jnp.where(c < r, 1.0, 0.0).astype(BF16)
    cum = _dot(strict, onehot.astype(BF16)) + base_ref[0:1, :]
    rank1 = jnp.sum(jnp.where(lane == e1, cum, 0.0), axis=-1, keepdims=True).astype(I32)
    rank2 = jnp.sum(jnp.where(lane == e2, cum, 0.0), axis=-1, keepdims=True).astype(I32)
    total = base_ref[0:1, :] + jnp.sum(onehot, axis=0, keepdims=True)
    base_ref[...] = jnp.broadcast_to(total, base_ref.shape)
    cnt_ref[...] = jnp.broadcast_to(total, cnt_ref.shape).astype(I32)

    ids = jnp.where(lane == 0, e1, jnp.where(lane == 1, e2, jnp.where(lane == 2, rank1,
                    jnp.where(lane == 3, rank2, 0))))
    ids_ref[...] = ids
    gates_ref[...] = jnp.where(lane == 0, g1, jnp.where(lane == 1, g2, 0.0))


def moe_route(logits, bias, tm):
    n = logits.shape[0]
    return pl.pallas_call(
        _route_kernel,
        grid=(n // tm,),
        in_specs=[
            pl.BlockSpec((tm, LANES), lambda i: (i, 0)),
            pl.BlockSpec((1, LANES), lambda i: (0, 0)),
        ],
        out_specs=[
            pl.BlockSpec((tm, LANES), lambda i: (i, 0)),
            pl.BlockSpec((tm, LANES), lambda i: (i, 0)),
            pl.BlockSpec((8, LANES), lambda i: (0, 0)),
        ],
        out_shape=[
            jax.ShapeDtypeStruct((n, LANES), I32),
            jax.ShapeDtypeStruct((n, LANES), F32),
            jax.ShapeDtypeStruct((8, LANES), I32),
        ],
        scratch_shapes=[pltpu.VMEM((8, LANES), F32)],
        compiler_params=_cparams(("arbitrary",)),
        name="moe_route",
    )(logits, bias)


def _dispatch_kernel(zflag_ref, dest_ref, h_ref, xb_ref, zbuf_ref, sem, zsem):
    tm = h_ref.shape[0]
    n_blocks = xb_ref.shape[0] // MOE_ROWS

    @pl.when(pl.program_id(0) == 0)
    def _():
        zbuf_ref[...] = jnp.zeros_like(zbuf_ref)

        def zero_copy(b):
            r0 = pl.multiple_of(b * MOE_ROWS, MOE_ROWS)
            return pltpu.make_async_copy(zbuf_ref, xb_ref.at[pl.ds(r0, MOE_ROWS)], zsem)

        def zstart(b, carry):
            @pl.when(zflag_ref[b] > 0)
            def _():
                zero_copy(b).start()
            return carry

        def zwait(b, carry):
            @pl.when(zflag_ref[b] > 0)
            def _():
                zero_copy(b).wait()
            return carry

        lax.fori_loop(0, n_blocks, zstart, 0)
        lax.fori_loop(0, n_blocks, zwait, 0)

    def issue(r, carry):
        for s in range(MOE_TOP_K):
            d = dest_ref[0, 0, MOE_TOP_K * r + s]
            pltpu.make_async_copy(h_ref.at[pl.ds(r, 1)], xb_ref.at[pl.ds(d, 1)], sem).start()
        return carry

    lax.fori_loop(0, tm, issue, 0, unroll=DMA_UNROLL)

    def drain(r, carry):
        for s in range(MOE_TOP_K):
            pltpu.make_async_copy(h_ref.at[pl.ds(0, 1)], xb_ref.at[pl.ds(0, 1)], sem).wait()
        return carry

    lax.fori_loop(0, tm, drain, 0, unroll=DMA_UNROLL)


def moe_dispatch(hp, dest, zflag, n_rows, tm):
    n, dh = hp.shape
    dest3 = dest.reshape(n // tm, 1, MOE_TOP_K * tm)
    return pl.pallas_call(
        _dispatch_kernel,
        grid=(n // tm,),
        in_specs=[
            pl.BlockSpec(memory_space=pltpu.SMEM),
            pl.BlockSpec((1, 1, MOE_TOP_K * tm), lambda i: (i, 0, 0), memory_space=pltpu.SMEM),
            pl.BlockSpec((tm, dh), lambda i: (i, 0)),
        ],
        out_specs=pl.BlockSpec(memory_space=pl.ANY),
        out_shape=jax.ShapeDtypeStruct((n_rows, dh), U32),
        scratch_shapes=[pltpu.VMEM((MOE_ROWS, dh), U32), pltpu.SemaphoreType.DMA(()),
                        pltpu.SemaphoreType.DMA(())],
        compiler_params=_cparams(("arbitrary",)),
        name="moe_dispatch",
    )(zflag, dest3, hp)


def _experts_kernel(be_ref, nb_ref, x_ref, wg_ref, wu_ref, wd_ref, y_ref, wgu_s, wd_s):
    i = pl.program_id(0)
    ff = wg_ref.shape[1]
    active = i < nb_ref[0]
    new_expert = (i == 0) | (be_ref[i] != be_ref[jnp.maximum(i - 1, 0)])

    @pl.when(active & new_expert)
    def _():
        wgu_s[:, :ff] = wg_ref[...].astype(BF16)
        wgu_s[:, ff:] = wu_ref[...].astype(BF16)
        wd_s[...] = wd_ref[...].astype(BF16)

    @pl.when(active)
    def _():
        xa, xb = _unpack_bf16_pair(x_ref[...])
        half = xa.shape[1]
        h = _dot(xa.astype(BF16), wgu_s[:half, :]) + _dot(xb.astype(BF16), wgu_s[half:, :])
        act = (_silu(h[:, :ff]) * h[:, ff:]).astype(BF16)
        y = _dot(act, wd_s[...])
        y_ref[...] = _pack_bf16_pair(y[:, :half], y[:, half:])

    @pl.when(jnp.logical_not(active))
    def _():
        y_ref[...] = jnp.zeros_like(y_ref)


def moe_experts(xbuf, block_e, n_used, w_gate, w_up, w_down):
    n_rows, dh = xbuf.shape
    nb = n_rows // MOE_ROWS
    d, ff = w_gate.shape[1], w_gate.shape[2]
    xmap = lambda i, be, nu: (jnp.minimum(i, nu[0] - 1), 0)
    grid_spec = pltpu.PrefetchScalarGridSpec(
        num_scalar_prefetch=2,
        grid=(nb,),
        in_specs=[
            pl.BlockSpec((MOE_ROWS, dh), xmap),
            pl.BlockSpec((None, d, ff), lambda i, be, nu: (be[i], 0, 0)),
            pl.BlockSpec((None, d, ff), lambda i, be, nu: (be[i], 0, 0)),
            pl.BlockSpec((None, ff, d), lambda i, be, nu: (be[i], 0, 0)),
        ],
        out_specs=pl.BlockSpec((MOE_ROWS, dh), lambda i, be, nu: (i, 0)),
        scratch_shapes=[pltpu.VMEM((d, 2 * ff), BF16), pltpu.VMEM((ff, d), BF16)],
    )
    return pl.pallas_call(
        _experts_kernel,
        grid_spec=grid_spec,
        out_shape=jax.ShapeDtypeStruct((n_rows, dh), U32),
        compiler_params=_cparams(("arbitrary",)),
        name="moe_experts",
    )(block_e, n_used, xbuf, w_gate, w_up, w_down)


def _combine_kernel(dest_ref, x_ref, gates_ref, nw_ref, yb_ref, o_ref, buf_ref, sem, *, final_norm):
    tm = x_ref.shape[0]

    def issue(r, carry):
        for s in range(MOE_TOP_K):
            d = dest_ref[0, 0, MOE_TOP_K * r + s]
            pltpu.make_async_copy(yb_ref.at[pl.ds(d, 1)], buf_ref.at[s, pl.ds(r, 1)], sem).start()
        return carry

    lax.fori_loop(0, tm, issue, 0, unroll=DMA_UNROLL)

    def drain(r, carry):
        for s in range(MOE_TOP_K):
            pltpu.make_async_copy(yb_ref.at[pl.ds(0, 1)], buf_ref.at[0, pl.ds(0, 1)], sem).wait()
        return carry

    lax.fori_loop(0, tm, drain, 0, unroll=DMA_UNROLL)

    gates = gates_ref[...]
    a1, b1 = _unpack_bf16_pair(buf_ref[0])
    a2, b2 = _unpack_bf16_pair(buf_ref[1])
    g1 = gates[:, 0:1]
    g2 = gates[:, 1:2]
    x = x_ref[...]
    half = a1.shape[1]
    xn = jnp.concatenate([x[:, :half] + g1 * a1 + g2 * a2, x[:, half:] + g1 * b1 + g2 * b2], axis=1)
    if final_norm:
        xn = xn * lax.rsqrt(jnp.mean(xn * xn, axis=-1, keepdims=True) + EPS) * nw_ref[...]
    o_ref[...] = xn


def moe_combine(x, ybuf, dest, gates, nw, tm, final_norm):
    n, d = x.shape
    dh = d // 2
    dest3 = dest.reshape(n // tm, 1, MOE_TOP_K * tm)
    kern = functools.partial(_combine_kernel, final_norm=final_norm)
    return pl.pallas_call(
        kern,
        grid=(n // tm,),
        in_specs=[
            pl.BlockSpec((1, 1, MOE_TOP_K * tm), lambda i: (i, 0, 0), memory_space=pltpu.SMEM),
            pl.BlockSpec((tm, d), lambda i: (i, 0)),
            pl.BlockSpec((tm, LANES), lambda i: (i, 0)),
            pl.BlockSpec((1, d), lambda i: (0, 0)),
            pl.BlockSpec(memory_space=pl.ANY),
        ],
        out_specs=pl.BlockSpec((tm, d), lambda i: (i, 0)),
        out_shape=jax.ShapeDtypeStruct((n, d), F32),
        scratch_shapes=[pltpu.VMEM((MOE_TOP_K, tm, dh), U32), pltpu.SemaphoreType.DMA(())],
        compiler_params=_cparams(("arbitrary",)),
        name="moe_combine",
    )(dest3, x, gates, nw.reshape(1, d), ybuf)


def hier_moe(x, hp, logits, b_group, b_expert, w_gate, w_up, w_down, nw_final, final_norm, tiles):
    n = x.shape[0]
    E = MOE_N_EXPERTS
    bias = jnp.zeros((1, LANES), F32).at[0, :MOE_GROUPS].set(b_group).at[0, MOE_GROUPS:MOE_GROUPS + E].set(b_expert)
    ids, gates, cnt = moe_route(logits, bias, tiles["route"])
    counts = cnt[0, :E]
    pcounts = (counts + MOE_ROWS - 1) // MOE_ROWS * MOE_ROWS
    pends = jnp.cumsum(pcounts)
    poffs = pends - pcounts
    n_blocks = (n * MOE_TOP_K) // MOE_ROWS + E
    eidx = jnp.arange(E, dtype=I32)
    sel_off = jnp.sum(jnp.where(ids[:, 0:MOE_TOP_K, None] == eidx, poffs, 0), axis=-1)
    dest = sel_off + ids[:, MOE_TOP_K:2 * MOE_TOP_K]
    blk_start = jnp.arange(n_blocks, dtype=I32) * MOE_ROWS
    n_used = (pends[-1] // MOE_ROWS).astype(I32)
    block_e = jnp.minimum(jnp.sum((pends[None, :] <= blk_start[:, None]).astype(I32), axis=1), E - 1)
    last_e = jnp.sum(jnp.where(jnp.arange(n_blocks) == n_used - 1, block_e, 0))
    block_e = jnp.where(jnp.arange(n_blocks) < n_used, block_e, last_e).astype(I32)
    blk_end = blk_start + MOE_ROWS
    is_last = jnp.any((pends[None, :] == blk_end[:, None]) & (counts[None, :] > 0), axis=1)
    zflag = (is_last | (jnp.arange(n_blocks) >= n_used)).astype(I32)
    xbuf = moe_dispatch(hp, dest, zflag, n_blocks * MOE_ROWS, tiles["dispatch"])
    ybuf = moe_experts(xbuf, block_e, n_used.reshape(1), w_gate, w_up, w_down)
    return moe_combine(x, ybuf, dest, gates, nw_final, tiles["combine"], final_norm)


def _tiles(n, tlen):
    pick = lambda pref, total: next(t for t in (pref, 512, 256, 128, 64) if t <= pref and total % t == 0)
    return {
        "mm_m": pick(1024, n), "mm_n": 1024, "out_m": pick(256, n),
        "ssd_t": pick(512, tlen), "gla_t": pick(512, tlen),
        "route": pick(512, n), "dispatch": pick(256, n), "combine": pick(256, n),
    }


def kernel(x, ssd_w_in, ssd_conv_w, ssd_conv_b, ssd_dt_bias, ssd_a_log, ssd_d, ssd_norm_w, ssd_w_out,
           gla_w_in, gla_w_a2, gla_b_a, gla_norm_w, gla_w_out, norm_mix, norm_ffn, moe_w_group,
           moe_b_group, moe_w_expert, moe_b_expert, moe_w_gate, moe_w_up, moe_w_down, norm_final):
    bsz, tlen, d = x.shape
    n = bsz * tlen
    depth = norm_mix.shape[0]
    tiles = _tiles(n, tlen)
    xf = x.reshape(n, d)
    for i in range(depth):
        j = i // 2
        if i % 2 == 0:
            w_in = ssd_w_in[j]
            m_main = w_in.shape[1] - ssd_dt_bias.shape[1]
            proj, dtraw = norm_matmul(xf, norm_mix[i], w_in.astype(BF16), m_main,
                                      _small_weight(w_in[:, m_main:]), tiles["mm_m"], tiles["mm_n"])
            y = ssd_scan(proj, dtraw, ssd_conv_w[j], ssd_conv_b[j], ssd_dt_bias[j], ssd_a_log[j],
                         ssd_d[j], ssd_norm_w[j], bsz, tlen, tiles["ssd_t"])
            w_out = ssd_w_out[j]
        else:
            w_in = gla_w_in[j]
            m_main = w_in.shape[1] - GLA_GATE_RANK
            proj, alr = norm_matmul(xf, norm_mix[i], w_in.astype(BF16), m_main,
                                    _small_weight(w_in[:, m_main:]), tiles["mm_m"], tiles["mm_n"])
            y = gla_scan(proj, alr, gla_w_a2[j], gla_b_a[j], gla_norm_w[j], bsz, tlen, tiles["gla_t"])
            w_out = gla_w_out[j]
        w_router = _small_weight(jnp.concatenate([moe_w_group[i], moe_w_expert[i]], axis=1))
        xf, hp, logits = out_proj(y, w_out.astype(BF16), xf, norm_ffn[i], w_router, tiles["out_m"])
        xf = hier_moe(xf, hp, logits, moe_b_group[i], moe_b_expert[i], moe_w_gate[i], moe_w_up[i],
                      moe_w_down[i], norm_final, i == depth - 1, tiles)
    return xf.reshape(bsz, tlen, d)
```

```python
import functools

import numpy as np
import jax
import jax.numpy as jnp
from jax import lax
from jax.experimental import pallas as pl
from jax.experimental.pallas import tpu as pltpu

F32 = jnp.float32
BF16 = jnp.bfloat16
U32 = jnp.uint32
I32 = jnp.int32

EPS = 1e-6
LANES = 128
BF16_ROWS = 16
VMEM_LIMIT = 56 * 1024 * 1024

SSD_HEAD_DIM = 64
SSD_HEADS_PER_GROUP = 8
SSD_D_STATE = 128
SSD_CONV = 4
SSD_CHUNK = 128
SSD_GROUP_W = SSD_HEAD_DIM * SSD_HEADS_PER_GROUP

GLA_N_HEADS = 4
GLA_GATE_RANK = 16
GLA_GATE_TAU = 16.0
GLA_CHUNK = 128
GLA_LEVELS = 7

MOE_GROUPS = 8
MOE_EXPERTS_PER_GROUP = 8
MOE_N_EXPERTS = 64
MOE_TOP_K = 2
MOE_ROWS = 256
DMA_UNROLL = 8


def _cparams(sem):
    return pltpu.CompilerParams(dimension_semantics=sem, vmem_limit_bytes=VMEM_LIMIT)


def _split_hi_lo(v):
    hi = v.astype(BF16)
    lo = (v - hi.astype(F32)).astype(BF16)
    return hi, lo


def _dot(a, b):
    return jnp.dot(a, b, preferred_element_type=F32)


def _dot_tb(a, b):
    return lax.dot_general(a, b, (((1,), (1,)), ((), ())), preferred_element_type=F32)


def _dot_ta(a, b):
    return lax.dot_general(a, b, (((0,), (0,)), ((), ())), preferred_element_type=F32)


def _silu(v):
    hv = 0.5 * v
    return hv + hv * jnp.tanh(hv)


def _pack_bf16_pair(a, b):
    ua = lax.bitcast_convert_type(a.astype(BF16).astype(F32), U32)
    ub = lax.bitcast_convert_type(b.astype(BF16).astype(F32), U32)
    return ua | (ub >> 16)


def _unpack_bf16_pair(u):
    a = lax.bitcast_convert_type(u & jnp.uint32(0xFFFF0000), F32)
    b = lax.bitcast_convert_type(u << 16, F32)
    return a, b


def _store_rows(ref, v):
    ns = v.shape[1] // LANES
    for s in range(ns):
        ref[pl.ds(s, v.shape[0], stride=ns), :] = v[:, s * LANES:(s + 1) * LANES]


def _load_rows(ref, ns):
    rows = ref.shape[0] // ns
    return jnp.concatenate([ref[pl.ds(s, rows, stride=ns), :] for s in range(ns)], axis=1)


def _norm_matmul_kernel(x_ref, nw_ref, w_ref, ws_ref, o_ref, os_ref, h_ref):
    j = pl.program_id(1)

    @pl.when(j == 0)
    def _():
        x = x_ref[...]
        h = x * lax.rsqrt(jnp.mean(x * x, axis=-1, keepdims=True) + EPS) * nw_ref[...]
        hi, lo = _split_hi_lo(h)
        h_ref[...] = hi
        s = _dot(hi, ws_ref[...])
        os_ref[...] = s[:, :LANES] + s[:, LANES:] + _dot(lo, ws_ref[:, :LANES])

    o_ref[...] = _dot(h_ref[...], w_ref[...]).astype(o_ref.dtype)


def _small_weight(w):
    k, n = w.shape
    wp = jnp.zeros((k, LANES), F32).at[:, :n].set(w)
    hi, lo = _split_hi_lo(wp)
    return jnp.concatenate([hi, lo], axis=1)


def norm_matmul(x, nw, w_main, layer, m, w_small, tm, tn):
    n, d = x.shape
    return pl.pallas_call(
        _norm_matmul_kernel,
        grid=(n // tm, m // tn),
        in_specs=[
            pl.BlockSpec((tm, d), lambda i, j: (i, 0)),
            pl.BlockSpec((1, d), lambda i, j: (0, 0)),
            pl.BlockSpec((None, d, tn), lambda i, j: (layer, 0, j)),
            pl.BlockSpec((d, 2 * LANES), lambda i, j: (0, 0)),
        ],
        out_specs=[
            pl.BlockSpec((tm, tn), lambda i, j: (i, j)),
            pl.BlockSpec((tm, LANES), lambda i, j: (i, 0)),
        ],
        out_shape=[
            jax.ShapeDtypeStruct((n, m), BF16),
            jax.ShapeDtypeStruct((n, LANES), F32),
        ],
        scratch_shapes=[pltpu.VMEM((tm, d), BF16)],
        compiler_params=_cparams(("parallel", "arbitrary")),
        name="norm_matmul",
    )(x, nw.reshape(1, d), w_main, w_small)


def _out_proj_kernel(y_ref, w_ref, x_ref, nw_ref, wr_ref, xo_ref, hp_ref, lg_ref):
    xn = x_ref[...] + _dot(y_ref[...], w_ref[...])
    xo_ref[...] = xn
    h = xn * lax.rsqrt(jnp.mean(xn * xn, axis=-1, keepdims=True) + EPS) * nw_ref[...]
    half = h.shape[1] // 2
    _store_rows(hp_ref, _pack_bf16_pair(h[:, :half], h[:, half:]))
    hi, lo = _split_hi_lo(h)
    s = _dot(hi, wr_ref[...])
    lg_ref[...] = s[:, :LANES] + s[:, LANES:] + _dot(lo, wr_ref[:, :LANES])


def out_proj(y, w_out, layer, x, nw, w_router, tm):
    n, k = y.shape
    d = x.shape[1]
    return pl.pallas_call(
        _out_proj_kernel,
        grid=(n // tm,),
        in_specs=[
            pl.BlockSpec((tm, k), lambda i: (i, 0)),
            pl.BlockSpec((None, k, d), lambda i: (layer, 0, 0), pipeline_mode=pl.Buffered(1)),
            pl.BlockSpec((tm, d), lambda i: (i, 0)),
            pl.BlockSpec((1, d), lambda i: (0, 0)),
            pl.BlockSpec((d, 2 * LANES), lambda i: (0, 0)),
        ],
        out_specs=[
            pl.BlockSpec((tm, d), lambda i: (i, 0)),
            pl.BlockSpec((tm * (d // 2 // LANES), LANES), lambda i: (i, 0)),
            pl.BlockSpec((tm, LANES), lambda i: (i, 0)),
        ],
        out_shape=[
            jax.ShapeDtypeStruct((n, d), F32),
            jax.ShapeDtypeStruct((n * (d // 2 // LANES), LANES), U32),
            jax.ShapeDtypeStruct((n, LANES), F32),
        ],
        compiler_params=_cparams(("parallel",)),
        name="out_proj",
    )(y, w_out, x, nw.reshape(1, d), w_router)


def _ssd_kernel(z_ref, xs_ref, b_ref, c_ref, dt_ref, selc_ref, selr_ref,
                cwx_ref, cwb_ref, cwc_ref, cbx_ref, cbb_ref, cbc_ref,
                bias_ref, biast_ref, a_ref, at_ref, dsk_ref, nw_ref,
                y_ref, state_ref, tail_ref, *, n_chunks):
    L = SSD_CHUNK
    W = SSD_GROUP_W
    NS = SSD_D_STATE
    HG = SSD_HEADS_PER_GROUP
    P = SSD_HEAD_DIM
    TAIL = BF16_ROWS
    WC = W + 2 * NS
    tb = n_chunks * L

    @pl.when(pl.program_id(2) == 0)
    def _():
        state_ref[...] = jnp.zeros_like(state_ref)
        tail_ref[...] = jnp.zeros_like(tail_ref)

    row = lax.broadcasted_iota(I32, (L, L), 0)
    col = lax.broadcasted_iota(I32, (L, L), 1)
    causal = row >= col
    tri = jnp.where(causal, 1.0, 0.0).astype(BF16)
    trit = jnp.where(col >= row, 1.0, 0.0).astype(BF16)
    er = lax.broadcasted_iota(I32, (2 * HG, W), 0)
    ec = lax.broadcasted_iota(I32, (2 * HG, W), 1)
    expand = jnp.where((er % HG) == (ec // P), 1.0, 0.0).astype(BF16)
    sr = lax.broadcasted_iota(I32, ((SSD_CONV - 1) * L, L + TAIL), 0)
    sc = lax.broadcasted_iota(I32, ((SSD_CONV - 1) * L, L + TAIL), 1)
    shift = jnp.where(sc == (sr % L) + (sr // L) + (TAIL - SSD_CONV + 1), 1.0, 0.0).astype(BF16)

    cw = jnp.concatenate([cwx_ref[...], cwb_ref[...], cwc_ref[...]], axis=1)
    cb = jnp.concatenate([cbx_ref[...], cbb_ref[...], cbc_ref[...]], axis=1)
    a_row = a_ref[...]
    a_col = at_ref[...]

    def expand_heads(v):
        hi, lo = _split_hi_lo(v)
        return _dot(jnp.concatenate([hi, lo], axis=1), expand)

    d = dt_ref[...]
    d1 = d.astype(BF16)
    r1 = d - d1.astype(F32)
    d2 = r1.astype(BF16)
    d3 = (r1 - d2.astype(F32)).astype(BF16)
    dcat = jnp.concatenate([d1, d2, d3], axis=0)
    c3 = _dot(dcat, selc_ref[...])
    dt_all = jax.nn.softplus(c3[:tb] + c3[tb:2 * tb] + c3[2 * tb:] + bias_ref[...])
    r3 = _dot_tb(selr_ref[...], dcat)
    dtt_all = jax.nn.softplus(r3[:, :tb] + r3[:, tb:2 * tb] + r3[:, 2 * tb:] + biast_ref[...])

    u_all = jnp.concatenate([xs_ref[...], b_ref[...], c_ref[...]], axis=1)
    ucat = jnp.concatenate([tail_ref[...], u_all], axis=0)
    tail_ref[...] = u_all[tb - TAIL:, :]

    for ci in range(n_chunks):
        r0 = ci * L
        shifted = _dot(shift, ucat[r0:r0 + L + TAIL, :])
        acc = cb + u_all[r0:r0 + L, :].astype(F32) * cw[SSD_CONV - 1:SSD_CONV, :]
        for k in range(SSD_CONV - 1):
            acc = acc + shifted[k * L:(k + 1) * L, :] * cw[k:k + 1, :]
        conv = _silu(acc)
        xc = conv[:, :W]
        bc = conv[:, W:W + NS].astype(BF16)
        cc = conv[:, W + NS:].astype(BF16)

        dt = dt_all[r0:r0 + L, :]
        da_hi, da_lo = _split_hi_lo(dt * a_row)
        acs2 = _dot(tri, jnp.concatenate([da_hi, da_lo], axis=1))
        acs = acs2[:, :HG] + acs2[:, HG:]
        dat_hi, dat_lo = _split_hi_lo(dtt_all[:, r0:r0 + L] * a_col)
        acst2 = _dot(jnp.concatenate([dat_hi, dat_lo], axis=0), trit)
        acst = acst2[:HG, :] + acst2[HG:, :]

        acs_last = acs[L - 1:L, :]
        dt_e = expand_heads(dt)
        eacs_e = expand_heads(jnp.exp(acs))
        wend_e = expand_heads(dt * jnp.exp(acs_last - acs))
        xdt_b = (xc * dt_e).astype(BF16)

        cbm = _dot_tb(cc, bc)
        state = state_ref[...]
        y_off = _dot(cc, state.astype(BF16)) * eacs_e
        ys = []
        for h in range(HG):
            seg = acs[:, h:h + 1] - acst[h:h + 1, :]
            m = jnp.where(causal, jnp.exp(seg), 0.0) * cbm
            ys.append(_dot(m.astype(BF16), xdt_b[:, h * P:(h + 1) * P]))
        y = jnp.concatenate(ys, axis=1) + y_off + xc * dsk_ref[...]
        state_ref[...] = state * eacs_e[L - 1:L, :] + _dot_ta(bc, (xc * wend_e).astype(BF16))

        yg = y * _silu(z_ref[r0:r0 + L, :].astype(F32))
        yn = yg * lax.rsqrt(jnp.mean(yg * yg, axis=-1, keepdims=True) + EPS) * nw_ref[...]
        y_ref[r0:r0 + L, :] = yn.astype(y_ref.dtype)


def ssd_scan(proj, dtraw, conv_w, conv_b, dt_bias, a_log, d_skip, norm_w, bsz, tlen, tb):
    n = bsz * tlen
    n_heads = dt_bias.shape[0]
    g = n_heads // SSD_HEADS_PER_GROUP
    di = g * SSD_GROUP_W
    W, NS, HG = SSD_GROUP_W, SSD_D_STATE, SSD_HEADS_PER_GROUP
    nt = tlen // tb
    lane = np.arange(LANES)
    sel = (lane[None, :, None] == (np.arange(g)[:, None, None] * HG + np.arange(HG)[None, None, :]))
    selc = jnp.asarray(sel, BF16)
    selr = jnp.asarray(sel.transpose(0, 2, 1), BF16)
    bias = dt_bias.reshape(g, 1, HG)
    biast = dt_bias.reshape(g, HG, 1)
    a = -jnp.exp(a_log.astype(F32))
    a_row = a.reshape(g, 1, HG)
    a_col = a.reshape(g, HG, 1)
    dsk = jnp.repeat(d_skip.astype(F32), SSD_HEAD_DIM).reshape(1, di)
    cb = conv_b.reshape(1, -1)
    xoff = di // W
    boff = 2 * di // NS
    coff = boff + g
    rowblk = lambda b, gi, t: b * nt + t
    kern = functools.partial(_ssd_kernel, n_chunks=tb // SSD_CHUNK)
    return pl.pallas_call(
        kern,
        grid=(bsz, g, nt),
        in_specs=[
            pl.BlockSpec((tb, W), lambda b, gi, t: (rowblk(b, gi, t), gi)),
            pl.BlockSpec((tb, W), lambda b, gi, t: (rowblk(b, gi, t), xoff + gi)),
            pl.BlockSpec((tb, NS), lambda b, gi, t: (rowblk(b, gi, t), boff + gi)),
            pl.BlockSpec((tb, NS), lambda b, gi, t: (rowblk(b, gi, t), coff + gi)),
            pl.BlockSpec((tb, LANES), lambda b, gi, t: (rowblk(b, gi, t), 0)),
            pl.BlockSpec((None, LANES, HG), lambda b, gi, t: (gi, 0, 0)),
            pl.BlockSpec((None, HG, LANES), lambda b, gi, t: (gi, 0, 0)),
            pl.BlockSpec((SSD_CONV, W), lambda b, gi, t: (0, gi)),
            pl.BlockSpec((SSD_CONV, NS), lambda b, gi, t: (0, di // NS + gi)),
            pl.BlockSpec((SSD_CONV, NS), lambda b, gi, t: (0, di // NS + g + gi)),
            pl.BlockSpec((1, W), lambda b, gi, t: (0, gi)),
            pl.BlockSpec((1, NS), lambda b, gi, t: (0, di // NS + gi)),
            pl.BlockSpec((1, NS), lambda b, gi, t: (0, di // NS + g + gi)),
            pl.BlockSpec((None, 1, HG), lambda b, gi, t: (gi, 0, 0)),
            pl.BlockSpec((None, HG, 1), lambda b, gi, t: (gi, 0, 0)),
            pl.BlockSpec((None, 1, HG), lambda b, gi, t: (gi, 0, 0)),
            pl.BlockSpec((None, HG, 1), lambda b, gi, t: (gi, 0, 0)),
            pl.BlockSpec((1, W), lambda b, gi, t: (0, gi)),
            pl.BlockSpec((1, W), lambda b, gi, t: (0, gi)),
        ],
        out_specs=pl.BlockSpec((tb, W), lambda b, gi, t: (rowblk(b, gi, t), gi)),
        out_shape=jax.ShapeDtypeStruct((n, di), BF16),
        scratch_shapes=[
            pltpu.VMEM((NS, W), F32),
            pltpu.VMEM((BF16_ROWS, W + 2 * NS), BF16),
        ],
        compiler_params=_cparams(("parallel", "parallel", "arbitrary")),
        name="ssd_scan",
    )(proj, proj, proj, proj, dtraw, selc, selr, conv_w, conv_w, conv_w, cb, cb, cb,
      bias, biast, a_row, a_col, dsk, norm_w.reshape(1, di))


def _gla_tables():
    c = GLA_CHUNK
    r = np.arange(c)
    d = np.zeros((GLA_LEVELS + 2, c, c), np.float32)
    masks = np.zeros((GLA_LEVELS + 1, c, c), np.float32)
    upper = np.zeros((GLA_LEVELS, c, 1), np.float32)
    masks[0] = np.eye(c)
    for j in range(1, GLA_LEVELS + 1):
        half = 1 << (j - 1)
        blk = r >> j
        mid = (blk << j) + half
        up = (r & ((1 << j) - 1)) >= half
        upper[j - 1, :, 0] = up
        for l in range(c):
            if up[l]:
                d[j - 1, l, mid[l]:l + 1] = 1.0
            else:
                d[j - 1, l, l + 1:mid[l]] = 1.0
        same = blk[:, None] == blk[None, :]
        masks[j] = same & up[:, None] & (~up)[None, :]
    d[GLA_LEVELS] = np.tril(np.ones((c, c)))
    d[GLA_LEVELS + 1] = np.triu(np.ones((c, c)), 1)
    return d.reshape(-1, c), masks, upper


def _gla_kernel(q_ref, k_ref, v_ref, g_ref, alr_ref, wa_ref, ba_ref, nw_ref,
                dall_ref, mask_ref, up_ref, o_ref, state_ref, *, n_chunks, hk):
    C = GLA_CHUNK
    scale = hk ** -0.5

    @pl.when(pl.program_id(2) == 0)
    def _():
        state_ref[...] = jnp.zeros_like(state_ref)

    wa = wa_ref[...]
    dall = dall_ref[...]

    a_hi, a_lo = _split_hi_lo(alr_ref[...])
    s = _dot(a_hi, wa)
    pre = s[:, :hk] + s[:, hk:] + _dot(a_lo, wa[:, :hk]) + ba_ref[...]
    lg = (jnp.minimum(pre, 0.0) - jnp.log1p(jnp.exp(-jnp.abs(pre)))) * (1.0 / GLA_GATE_TAU)
    lg_hi, lg_lo = _split_hi_lo(lg)
    lg2 = jnp.concatenate([lg_hi, lg_lo], axis=1)

    for ci in range(n_chunks):
        r0 = ci * C
        e2 = _dot(dall, lg2[r0:r0 + C, :])
        f = jnp.exp(e2[:, :hk] + e2[:, hk:])

        q = q_ref[r0:r0 + C, :].astype(F32) * scale
        k = k_ref[r0:r0 + C, :].astype(F32)
        v = v_ref[r0:r0 + C, :]
        sc = _dot_tb(q.astype(BF16), k.astype(BF16)) * mask_ref[0]
        for j in range(GLA_LEVELS):
            fj = f[j * C:(j + 1) * C, :]
            up = up_ref[j] > 0.5
            qj = jnp.where(up, q * fj, 0.0).astype(BF16)
            kj = jnp.where(up, 0.0, k * fj).astype(BF16)
            sc = sc + _dot_tb(qj, kj) * mask_ref[j + 1]
        f_cum = f[GLA_LEVELS * C:(GLA_LEVELS + 1) * C, :]
        f_end = f[(GLA_LEVELS + 1) * C:, :]
        state = state_ref[...]
        o = _dot(sc.astype(BF16), v) + _dot_tb((q * f_cum).astype(BF16), state.astype(BF16))
        state_ref[...] = state * f_cum[C - 1:C, :] + _dot_ta(v, (k * f_end).astype(BF16))

        on = o * lax.rsqrt(jnp.mean(o * o, axis=-1, keepdims=True) + EPS) * nw_ref[...]
        o_ref[r0:r0 + C, :] = (on * _silu(g_ref[r0:r0 + C, :].astype(F32))).astype(o_ref.dtype)


def gla_scan(proj, alr, w_a2, b_a, norm_w, bsz, tlen, tb):
    n = bsz * tlen
    dk = w_a2.shape[1]
    hk = dk // GLA_N_HEADS
    dv = (proj.shape[1] - 2 * dk) // 2
    hv = dv // GLA_N_HEADS
    nt = tlen // tb
    H = GLA_N_HEADS
    wa = jnp.zeros((LANES, dk), F32).at[:w_a2.shape[0], :].set(w_a2)
    wa_hi, wa_lo = _split_hi_lo(wa)
    wa_cat = jnp.concatenate([wa_hi.reshape(LANES, H, hk), wa_lo.reshape(LANES, H, hk)], axis=2)
    wa_cat = wa_cat.reshape(LANES, 2 * dk)
    d_all, masks, upper = _gla_tables()
    d_all = jnp.asarray(d_all, BF16)
    masks = jnp.asarray(masks, F32)
    upper = jnp.asarray(upper, F32)
    rowblk = lambda b, h, t: b * nt + t
    kern = functools.partial(_gla_kernel, n_chunks=tb // GLA_CHUNK, hk=hk)
    return pl.pallas_call(
        kern,
        grid=(bsz, H, nt),
        in_specs=[
            pl.BlockSpec((tb, hk), lambda b, h, t: (rowblk(b, h, t), h)),
            pl.BlockSpec((tb, hk), lambda b, h, t: (rowblk(b, h, t), H + h)),
            pl.BlockSpec((tb, hv), lambda b, h, t: (rowblk(b, h, t), 2 * dk // hv + h)),
            pl.BlockSpec((tb, hv), lambda b, h, t: (rowblk(b, h, t), 2 * dk // hv + H + h)),
            pl.BlockSpec((tb, LANES), lambda b, h, t: (rowblk(b, h, t), 0)),
            pl.BlockSpec((LANES, 2 * hk), lambda b, h, t: (0, h)),
            pl.BlockSpec((1, hk), lambda b, h, t: (0, h)),
            pl.BlockSpec((1, hv), lambda b, h, t: (0, 0)),
            pl.BlockSpec(d_all.shape, lambda b, h, t: (0, 0)),
            pl.BlockSpec(masks.shape, lambda b, h, t: (0, 0, 0)),
            pl.BlockSpec(upper.shape, lambda b, h, t: (0, 0, 0)),
        ],
        out_specs=pl.BlockSpec((tb, hv), lambda b, h, t: (rowblk(b, h, t), h)),
        out_shape=jax.ShapeDtypeStruct((n, dv), BF16),
        scratch_shapes=[pltpu.VMEM((hv, hk), F32)],
        compiler_params=_cparams(("parallel", "parallel", "arbitrary")),
        name="gla_scan",
    )(proj, proj, proj, proj, alr, wa_cat, b_a.reshape(1, dk), norm_w.reshape(1, hv),
      d_all, masks, upper)


def _route_kernel(lg_ref, bias_ref, ids_ref, gates_ref, cnt_ref, base_ref):
    tm = lg_ref.shape[0]
    G, EPG, E = MOE_GROUPS, MOE_EXPERTS_PER_GROUP, MOE_N_EXPERTS

    @pl.when(pl.program_id(0) == 0)
    def _():
        base_ref[...] = jnp.zeros_like(base_ref)

    lane = lax.broadcasted_iota(I32, (tm, LANES), 1)
    logits = lg_ref[...] + bias_ref[...]
    neg = jnp.float32(-jnp.inf)
    big = jnp.int32(LANES)
    is_g = lane < G
    lg_g = jnp.where(is_g, logits, neg)
    m_g = jnp.max(lg_g, axis=-1, keepdims=True)
    gi = jnp.min(jnp.where(lg_g == m_g, lane, big), axis=-1, keepdims=True)
    z_g = jnp.sum(jnp.where(is_g, jnp.exp(lg_g - m_g), 0.0), axis=-1, keepdims=True)
    grp_w = 1.0 / z_g
    in_grp = (lane >= G) & (lane < G + E) & (((lane - G) // EPG) == gi)
    lg_e = jnp.where(in_grp, logits, neg)
    m1 = jnp.max(lg_e, axis=-1, keepdims=True)
    l1 = jnp.min(jnp.where(lg_e == m1, lane, big), axis=-1, keepdims=True)
    lg_e2 = jnp.where(lane == l1, neg, lg_e)
    m2 = jnp.max(lg_e2, axis=-1, keepdims=True)
    l2 = jnp.min(jnp.where(lg_e2 == m2, lane, big), axis=-1, keepdims=True)
    p2 = jnp.exp(m2 - m1)
    g1 = grp_w / (1.0 + p2)
    g2 = grp_w * p2 / (1.0 + p2)
    e1 = l1 - G
    e2 = l2 - G
    onehot = jnp.where((lane == e1) | (lane == e2), 1.0, 0.0)
    r = lax.broadcasted_iota(I32, (tm, tm), 0)
    c = lax.broadcasted_iota(I32, (tm, tm), 1)
    strict = jnp.where(c < r, 1.0, 0.0).astype(BF16)
    cum = _dot(strict, onehot.astype(BF16)) + base_ref[0:1, :]
    rank1 = jnp.sum(jnp.where(lane == e1, cum, 0.0), axis=-1, keepdims=True).astype(I32)
    rank2 = jnp.sum(jnp.where(lane == e2, cum, 0.0), axis=-1, keepdims=True).astype(I32)
    total = base_ref[0:1, :] + jnp.sum(onehot, axis=0, keepdims=True)
    base_ref[...] = jnp.broadcast_to(total, base_ref.shape)
    cnt_ref[...] = jnp.broadcast_to(total, cnt_ref.shape).astype(I32)

    ids = jnp.where(lane == 0, e1, jnp.where(lane == 1, e2, jnp.where(lane == 2, rank1,
                    jnp.where(lane == 3, rank2, 0))))
    ids_ref[...] = ids
    gates_ref[...] = jnp.where(lane == 0, g1, jnp.where(lane == 1, g2, 0.0))


def moe_route(logits, bias, tm):
    n = logits.shape[0]
    return pl.pallas_call(
        _route_kernel,
        grid=(n // tm,),
        in_specs=[
            pl.BlockSpec((tm, LANES), lambda i: (i, 0)),
            pl.BlockSpec((1, LANES), lambda i: (0, 0)),
        ],
        out_specs=[
            pl.BlockSpec((tm, LANES), lambda i: (i, 0)),
            pl.BlockSpec((tm, LANES), lambda i: (i, 0)),
            pl.BlockSpec((8, LANES), lambda i: (0, 0)),
        ],
        out_shape=[
            jax.ShapeDtypeStruct((n, LANES), I32),
            jax.ShapeDtypeStruct((n, LANES), F32),
            jax.ShapeDtypeStruct((8, LANES), I32),
        ],
        scratch_shapes=[pltpu.VMEM((8, LANES), F32)],
        compiler_params=_cparams(("arbitrary",)),
        name="moe_route",
    )(logits, bias)


def _dispatch_kernel(zflag_ref, dest_ref, h_ref, xb_ref, zbuf_ref, sem, zsem, *, ns):
    tm = h_ref.shape[0] // ns
    blk = MOE_ROWS * ns
    n_blocks = xb_ref.shape[0] // blk

    @pl.when(pl.program_id(0) == 0)
    def _():
        zbuf_ref[...] = jnp.zeros_like(zbuf_ref)

        def zero_copy(b):
            r0 = pl.multiple_of(b * blk, blk)
            return pltpu.make_async_copy(zbuf_ref, xb_ref.at[pl.ds(r0, blk)], zsem)

        def zstart(b, carry):
            @pl.when(zflag_ref[b] > 0)
            def _():
                zero_copy(b).start()
            return carry

        def zwait(b, carry):
            @pl.when(zflag_ref[b] > 0)
            def _():
                zero_copy(b).wait()
            return carry

        lax.fori_loop(0, n_blocks, zstart, 0)
        lax.fori_loop(0, n_blocks, zwait, 0)

    def issue(r, carry):
        src = h_ref.at[pl.ds(pl.multiple_of(r * ns, ns), ns)]
        for s in range(MOE_TOP_K):
            d = pl.multiple_of(dest_ref[0, 0, MOE_TOP_K * r + s], ns)
            pltpu.make_async_copy(src, xb_ref.at[pl.ds(d, ns)], sem).start()
        return carry

    lax.fori_loop(0, tm, issue, 0, unroll=DMA_UNROLL)

    def drain(r, carry):
        for s in range(MOE_TOP_K):
            pltpu.make_async_copy(h_ref.at[pl.ds(0, ns)], xb_ref.at[pl.ds(0, ns)], sem).wait()
        return carry

    lax.fori_loop(0, tm, drain, 0, unroll=DMA_UNROLL)


def moe_dispatch(hp, dest, zflag, n_rows, ns, tm):
    n = hp.shape[0] // ns
    dest3 = dest.reshape(n // tm, 1, MOE_TOP_K * tm)
    return pl.pallas_call(
        functools.partial(_dispatch_kernel, ns=ns),
        grid=(n // tm,),
        in_specs=[
            pl.BlockSpec(memory_space=pltpu.SMEM),
            pl.BlockSpec((1, 1, MOE_TOP_K * tm), lambda i: (i, 0, 0), memory_space=pltpu.SMEM),
            pl.BlockSpec((tm * ns, LANES), lambda i: (i, 0)),
        ],
        out_specs=pl.BlockSpec(memory_space=pl.ANY),
        out_shape=jax.ShapeDtypeStruct((n_rows * ns, LANES), U32),
        scratch_shapes=[pltpu.VMEM((MOE_ROWS * ns, LANES), U32), pltpu.SemaphoreType.DMA(()),
                        pltpu.SemaphoreType.DMA(())],
        compiler_params=_cparams(("arbitrary",)),
        name="moe_dispatch",
    )(zflag, dest3, hp)


def _experts_kernel(be_ref, nb_ref, x_ref, wg_ref, wu_ref, wd_ref, y_ref, wgu_s, wd_s):
    i = pl.program_id(0)
    ff = wg_ref.shape[1]
    active = i < nb_ref[0]
    new_expert = (i == 0) | (be_ref[i] != be_ref[jnp.maximum(i - 1, 0)])

    @pl.when(active & new_expert)
    def _():
        wgu_s[:, :ff] = wg_ref[...].astype(BF16)
        wgu_s[:, ff:] = wu_ref[...].astype(BF16)
        wd_s[...] = wd_ref[...].astype(BF16)

    @pl.when(active)
    def _():
        xa, xb = _unpack_bf16_pair(_load_rows(x_ref, x_ref.shape[0] // MOE_ROWS))
        half = xa.shape[1]
        h = _dot(xa.astype(BF16), wgu_s[:half, :]) + _dot(xb.astype(BF16), wgu_s[half:, :])
        act = (_silu(h[:, :ff]) * h[:, ff:]).astype(BF16)
        y = _dot(act, wd_s[...])
        _store_rows(y_ref, _pack_bf16_pair(y[:, :half], y[:, half:]))

    @pl.when(jnp.logical_not(active))
    def _():
        y_ref[...] = jnp.zeros_like(y_ref)


def moe_experts(xbuf, block_e, n_used, w_gate, w_up, w_down, layer):
    d, ff = w_gate.shape[2], w_gate.shape[3]
    ns = d // 2 // LANES
    blk = MOE_ROWS * ns
    nb = xbuf.shape[0] // blk
    xmap = lambda i, be, nu: (jnp.minimum(i, nu[0] - 1), 0)
    wmap = lambda i, be, nu: (layer, be[i], 0, 0)
    grid_spec = pltpu.PrefetchScalarGridSpec(
        num_scalar_prefetch=2,
        grid=(nb,),
        in_specs=[
            pl.BlockSpec((blk, LANES), xmap),
            pl.BlockSpec((None, None, d, ff), wmap),
            pl.BlockSpec((None, None, d, ff), wmap),
            pl.BlockSpec((None, None, ff, d), wmap),
        ],
        out_specs=pl.BlockSpec((blk, LANES), lambda i, be, nu: (i, 0)),
        scratch_shapes=[pltpu.VMEM((d, 2 * ff), BF16), pltpu.VMEM((ff, d), BF16)],
    )
    return pl.pallas_call(
        _experts_kernel,
        grid_spec=grid_spec,
        out_shape=jax.ShapeDtypeStruct(xbuf.shape, U32),
        compiler_params=_cparams(("arbitrary",)),
        name="moe_experts",
    )(block_e, n_used, xbuf, w_gate, w_up, w_down)


def _combine_kernel(dest_ref, x_ref, gates_ref, nw_ref, yb_ref, o_ref, buf_ref, sem, *, final_norm):
    tm = x_ref.shape[0]
    ns = buf_ref.shape[1] // tm

    def issue(r, carry):
        r0 = pl.multiple_of(r * ns, ns)
        for s in range(MOE_TOP_K):
            d = pl.multiple_of(dest_ref[0, 0, MOE_TOP_K * r + s], ns)
            pltpu.make_async_copy(yb_ref.at[pl.ds(d, ns)], buf_ref.at[s, pl.ds(r0, ns)], sem).start()
        return carry

    lax.fori_loop(0, tm, issue, 0, unroll=DMA_UNROLL)

    def drain(r, carry):
        for s in range(MOE_TOP_K):
            pltpu.make_async_copy(yb_ref.at[pl.ds(0, ns)], buf_ref.at[0, pl.ds(0, ns)], sem).wait()
        return carry

    lax.fori_loop(0, tm, drain, 0, unroll=DMA_UNROLL)

    gates = gates_ref[...]
    a1, b1 = _unpack_bf16_pair(_load_rows(buf_ref.at[0], ns))
    a2, b2 = _unpack_bf16_pair(_load_rows(buf_ref.at[1], ns))
    g1 = gates[:, 0:1]
    g2 = gates[:, 1:2]
    x = x_ref[...]
    half = a1.shape[1]
    xn = jnp.concatenate([x[:, :half] + g1 * a1 + g2 * a2, x[:, half:] + g1 * b1 + g2 * b2], axis=1)
    if final_norm:
        xn = xn * lax.rsqrt(jnp.mean(xn * xn, axis=-1, keepdims=True) + EPS) * nw_ref[...]
    o_ref[...] = xn


def moe_combine(x, ybuf, dest, gates, nw, tm, final_norm):
    n, d = x.shape
    ns = d // 2 // LANES
    dest3 = dest.reshape(n // tm, 1, MOE_TOP_K * tm)
    kern = functools.partial(_combine_kernel, final_norm=final_norm)
    return pl.pallas_call(
        kern,
        grid=(n // tm,),
        in_specs=[
            pl.BlockSpec((1, 1, MOE_TOP_K * tm), lambda i: (i, 0, 0), memory_space=pltpu.SMEM),
            pl.BlockSpec((tm, d), lambda i: (i, 0)),
            pl.BlockSpec((tm, LANES), lambda i: (i, 0)),
            pl.BlockSpec((1, d), lambda i: (0, 0)),
            pl.BlockSpec(memory_space=pl.ANY),
        ],
        out_specs=pl.BlockSpec((tm, d), lambda i: (i, 0)),
        out_shape=jax.ShapeDtypeStruct((n, d), F32),
        scratch_shapes=[pltpu.VMEM((MOE_TOP_K, tm * ns, LANES), U32), pltpu.SemaphoreType.DMA(())],
        compiler_params=_cparams(("arbitrary",)),
        name="moe_combine",
    )(dest3, x, gates, nw.reshape(1, d), ybuf)


def hier_moe(x, hp, logits, b_group, b_expert, w_gate, w_up, w_down, layer, nw_final, final_norm, tiles):
    n = x.shape[0]
    E = MOE_N_EXPERTS
    bias = jnp.zeros((1, LANES), F32).at[0, :MOE_GROUPS].set(b_group).at[0, MOE_GROUPS:MOE_GROUPS + E].set(b_expert)
    ids, gates, cnt = moe_route(logits, bias, tiles["route"])
    counts = cnt[0, :E]
    pcounts = (counts + MOE_ROWS - 1) // MOE_ROWS * MOE_ROWS
    pends = jnp.cumsum(pcounts)
    poffs = pends - pcounts
    n_blocks = (n * MOE_TOP_K) // MOE_ROWS + E
    eidx = jnp.arange(E, dtype=I32)
    sel_off = jnp.sum(jnp.where(ids[:, 0:MOE_TOP_K, None] == eidx, poffs, 0), axis=-1)
    ns = x.shape[1] // 2 // LANES
    dest = (sel_off + ids[:, MOE_TOP_K:2 * MOE_TOP_K]) * ns
    blk_start = jnp.arange(n_blocks, dtype=I32) * MOE_ROWS
    n_used = (pends[-1] // MOE_ROWS).astype(I32)
    block_e = jnp.minimum(jnp.sum((pends[None, :] <= blk_start[:, None]).astype(I32), axis=1), E - 1)
    last_e = jnp.sum(jnp.where(jnp.arange(n_blocks) == n_used - 1, block_e, 0))
    block_e = jnp.where(jnp.arange(n_blocks) < n_used, block_e, last_e).astype(I32)
    blk_end = blk_start + MOE_ROWS
    is_last = jnp.any((pends[None, :] == blk_end[:, None]) & (counts[None, :] > 0), axis=1)
    zflag = (is_last | (jnp.arange(n_blocks) >= n_used)).astype(I32)
    xbuf = moe_dispatch(hp, dest, zflag, n_blocks * MOE_ROWS, ns, tiles["dispatch"])
    ybuf = moe_experts(xbuf, block_e, n_used.reshape(1), w_gate, w_up, w_down, layer)
    return moe_combine(x, ybuf, dest, gates, nw_final, tiles["combine"], final_norm)


def _tiles(n, tlen):
    pick = lambda pref, total: next(t for t in (pref, 512, 256, 128, 64) if t <= pref and total % t == 0)
    return {
        "mm_m": pick(1024, n), "mm_n": 1024, "out_m": pick(512, n),
        "ssd_t": pick(512, tlen), "gla_t": pick(512, tlen),
        "route": pick(512, n), "dispatch": pick(256, n), "combine": pick(256, n),
    }


def kernel(x, ssd_w_in, ssd_conv_w, ssd_conv_b, ssd_dt_bias, ssd_a_log, ssd_d, ssd_norm_w, ssd_w_out,
           gla_w_in, gla_w_a2, gla_b_a, gla_norm_w, gla_w_out, norm_mix, norm_ffn, moe_w_group,
           moe_b_group, moe_w_expert, moe_b_expert, moe_w_gate, moe_w_up, moe_w_down, norm_final):
    bsz, tlen, d = x.shape
    n = bsz * tlen
    depth = norm_mix.shape[0]
    tiles = _tiles(n, tlen)
    xf = x.reshape(n, d)
    ssd_w_in_b, ssd_w_out_b = ssd_w_in.astype(BF16), ssd_w_out.astype(BF16)
    gla_w_in_b, gla_w_out_b = gla_w_in.astype(BF16), gla_w_out.astype(BF16)
    for i in range(depth):
        j = i // 2
        if i % 2 == 0:
            m_main = ssd_w_in.shape[2] - ssd_dt_bias.shape[1]
            proj, dtraw = norm_matmul(xf, norm_mix[i], ssd_w_in_b, j, m_main,
                                      _small_weight(ssd_w_in[j, :, m_main:]), tiles["mm_m"], tiles["mm_n"])
            y = ssd_scan(proj, dtraw, ssd_conv_w[j], ssd_conv_b[j], ssd_dt_bias[j], ssd_a_log[j],
                         ssd_d[j], ssd_norm_w[j], bsz, tlen, tiles["ssd_t"])
            w_out = ssd_w_out_b
        else:
            m_main = gla_w_in.shape[2] - GLA_GATE_RANK
            proj, alr = norm_matmul(xf, norm_mix[i], gla_w_in_b, j, m_main,
                                    _small_weight(gla_w_in[j, :, m_main:]), tiles["mm_m"], tiles["mm_n"])
            y = gla_scan(proj, alr, gla_w_a2[j], gla_b_a[j], gla_norm_w[j], bsz, tlen, tiles["gla_t"])
            w_out = gla_w_out_b
        w_router = _small_weight(jnp.concatenate([moe_w_group[i], moe_w_expert[i]], axis=1))
        xf, hp, logits = out_proj(y, w_out, j, xf, norm_ffn[i], w_router, tiles["out_m"])
        xf = hier_moe(xf, hp, logits, moe_b_group[i], moe_b_expert[i], moe_w_gate, moe_w_up,
                      moe_w_down, i, norm_final, i == depth - 1, tiles)
    return xf.reshape(bsz, tlen, d)
```

```python
import functools

import numpy as np
import jax
import jax.numpy as jnp
from jax import lax
from jax.experimental import pallas as pl
from jax.experimental.pallas import tpu as pltpu

F32 = jnp.float32
BF16 = jnp.bfloat16
U32 = jnp.uint32
I32 = jnp.int32

EPS = 1e-6
LANES = 128
BF16_ROWS = 16
VMEM_LIMIT = 56 * 1024 * 1024

SSD_HEAD_DIM = 64
SSD_HEADS_PER_GROUP = 8
SSD_D_STATE = 128
SSD_CONV = 4
SSD_CHUNK = 128
SSD_GROUP_W = SSD_HEAD_DIM * SSD_HEADS_PER_GROUP

GLA_N_HEADS = 4
GLA_GATE_RANK = 16
GLA_GATE_TAU = 16.0
GLA_CHUNK = 128
GLA_LEVELS = 7

MOE_GROUPS = 8
MOE_EXPERTS_PER_GROUP = 8
MOE_N_EXPERTS = 64
MOE_TOP_K = 2
MOE_ROWS = 512
DMA_UNROLL = 8


def _cparams(sem):
    return pltpu.CompilerParams(dimension_semantics=sem, vmem_limit_bytes=VMEM_LIMIT)


def _split_hi_lo(v):
    hi = v.astype(BF16)
    lo = (v - hi.astype(F32)).astype(BF16)
    return hi, lo


def _dot(a, b):
    return jnp.dot(a, b, preferred_element_type=F32)


def _dot_tb(a, b):
    return lax.dot_general(a, b, (((1,), (1,)), ((), ())), preferred_element_type=F32)


def _dot_ta(a, b):
    return lax.dot_general(a, b, (((0,), (0,)), ((), ())), preferred_element_type=F32)


def _silu(v):
    hv = 0.5 * v
    return hv + hv * jnp.tanh(hv)


def _pack_bf16_pair(a, b):
    ua = lax.bitcast_convert_type(a.astype(BF16).astype(F32), U32)
    ub = lax.bitcast_convert_type(b.astype(BF16).astype(F32), U32)
    return ua | (ub >> 16)


def _unpack_bf16_pair(u):
    a = lax.bitcast_convert_type(u & jnp.uint32(0xFFFF0000), F32)
    b = lax.bitcast_convert_type(u << 16, F32)
    return a, b


def _store_rows(ref, v):
    ns = v.shape[1] // LANES
    for s in range(ns):
        ref[pl.ds(s, v.shape[0], stride=ns), :] = v[:, s * LANES:(s + 1) * LANES]


def _load_rows(ref, ns):
    rows = ref.shape[0] // ns
    return jnp.concatenate([ref[pl.ds(s, rows, stride=ns), :] for s in range(ns)], axis=1)


def _norm_matmul_kernel(x_ref, nw_ref, w_ref, ws_ref, o_ref, os_ref, h_ref):
    j = pl.program_id(1)

    @pl.when(j == 0)
    def _():
        x = x_ref[...]
        h = x * lax.rsqrt(jnp.mean(x * x, axis=-1, keepdims=True) + EPS) * nw_ref[...]
        hi, lo = _split_hi_lo(h)
        h_ref[...] = hi
        s = _dot(hi, ws_ref[...])
        os_ref[...] = s[:, :LANES] + s[:, LANES:] + _dot(lo, ws_ref[:, :LANES])

    o_ref[...] = _dot(h_ref[...], w_ref[...]).astype(o_ref.dtype)


def _small_weight(w):
    k, n = w.shape
    wp = jnp.zeros((k, LANES), F32).at[:, :n].set(w)
    hi, lo = _split_hi_lo(wp)
    return jnp.concatenate([hi, lo], axis=1)


def norm_matmul(x, nw, w_main, layer, m, w_small, tm, tn):
    n, d = x.shape
    return pl.pallas_call(
        _norm_matmul_kernel,
        grid=(n // tm, m // tn),
        in_specs=[
            pl.BlockSpec((tm, d), lambda i, j: (i, 0)),
            pl.BlockSpec((1, d), lambda i, j: (0, 0)),
            pl.BlockSpec((None, d, tn), lambda i, j: (layer, 0, j)),
            pl.BlockSpec((d, 2 * LANES), lambda i, j: (0, 0)),
        ],
        out_specs=[
            pl.BlockSpec((tm, tn), lambda i, j: (i, j)),
            pl.BlockSpec((tm, LANES), lambda i, j: (i, 0)),
        ],
        out_shape=[
            jax.ShapeDtypeStruct((n, m), BF16),
            jax.ShapeDtypeStruct((n, LANES), F32),
        ],
        scratch_shapes=[pltpu.VMEM((tm, d), BF16)],
        compiler_params=_cparams(("parallel", "arbitrary")),
        name="norm_matmul",
    )(x, nw.reshape(1, d), w_main, w_small)


def _out_proj_kernel(y_ref, w_ref, x_ref, nw_ref, wr_ref, xo_ref, hp_ref, lg_ref):
    xn = x_ref[...] + _dot(y_ref[...], w_ref[...])
    xo_ref[...] = xn
    h = xn * lax.rsqrt(jnp.mean(xn * xn, axis=-1, keepdims=True) + EPS) * nw_ref[...]
    half = h.shape[1] // 2
    _store_rows(hp_ref, _pack_bf16_pair(h[:, :half], h[:, half:]))
    hi, lo = _split_hi_lo(h)
    s = _dot(hi, wr_ref[...])
    lg_ref[...] = s[:, :LANES] + s[:, LANES:] + _dot(lo, wr_ref[:, :LANES])


def out_proj(y, w_out, layer, x, nw, w_router, tm):
    n, k = y.shape
    d = x.shape[1]
    return pl.pallas_call(
        _out_proj_kernel,
        grid=(n // tm,),
        in_specs=[
            pl.BlockSpec((tm, k), lambda i: (i, 0)),
            pl.BlockSpec((None, k, d), lambda i: (layer, 0, 0), pipeline_mode=pl.Buffered(1)),
            pl.BlockSpec((tm, d), lambda i: (i, 0)),
            pl.BlockSpec((1, d), lambda i: (0, 0)),
            pl.BlockSpec((d, 2 * LANES), lambda i: (0, 0)),
        ],
        out_specs=[
            pl.BlockSpec((tm, d), lambda i: (i, 0)),
            pl.BlockSpec((tm * (d // 2 // LANES), LANES), lambda i: (i, 0)),
            pl.BlockSpec((tm, LANES), lambda i: (i, 0)),
        ],
        out_shape=[
            jax.ShapeDtypeStruct((n, d), F32),
            jax.ShapeDtypeStruct((n * (d // 2 // LANES), LANES), U32),
            jax.ShapeDtypeStruct((n, LANES), F32),
        ],
        compiler_params=_cparams(("parallel",)),
        name="out_proj",
    )(y, w_out, x, nw.reshape(1, d), w_router)


def _ssd_kernel(z_ref, xs_ref, b_ref, c_ref, dt_ref, selc_ref, selr_ref,
                cwx_ref, cwb_ref, cwc_ref, cbx_ref, cbb_ref, cbc_ref,
                bias_ref, biast_ref, a_ref, at_ref, dsk_ref, nw_ref,
                y_ref, state_ref, tail_ref, *, n_chunks):
    L = SSD_CHUNK
    W = SSD_GROUP_W
    NS = SSD_D_STATE
    HG = SSD_HEADS_PER_GROUP
    P = SSD_HEAD_DIM
    TAIL = BF16_ROWS
    WC = W + 2 * NS
    tb = n_chunks * L

    @pl.when(pl.program_id(2) == 0)
    def _():
        state_ref[...] = jnp.zeros_like(state_ref)
        tail_ref[...] = jnp.zeros_like(tail_ref)

    row = lax.broadcasted_iota(I32, (L, L), 0)
    col = lax.broadcasted_iota(I32, (L, L), 1)
    causal = row >= col
    tri = jnp.where(causal, 1.0, 0.0).astype(BF16)
    trit = jnp.where(col >= row, 1.0, 0.0).astype(BF16)
    er = lax.broadcasted_iota(I32, (2 * HG, W), 0)
    ec = lax.broadcasted_iota(I32, (2 * HG, W), 1)
    expand = jnp.where((er % HG) == (ec // P), 1.0, 0.0).astype(BF16)
    sr = lax.broadcasted_iota(I32, ((SSD_CONV - 1) * L, L + TAIL), 0)
    sc = lax.broadcasted_iota(I32, ((SSD_CONV - 1) * L, L + TAIL), 1)
    shift = jnp.where(sc == (sr % L) + (sr // L) + (TAIL - SSD_CONV + 1), 1.0, 0.0).astype(BF16)

    cw = jnp.concatenate([cwx_ref[...], cwb_ref[...], cwc_ref[...]], axis=1)
    cb = jnp.concatenate([cbx_ref[...], cbb_ref[...], cbc_ref[...]], axis=1)
    a_row = a_ref[...]
    a_col = at_ref[...]

    def expand_heads(v):
        hi, lo = _split_hi_lo(v)
        return _dot(jnp.concatenate([hi, lo], axis=1), expand)

    d = dt_ref[...]
    d1 = d.astype(BF16)
    r1 = d - d1.astype(F32)
    d2 = r1.astype(BF16)
    d3 = (r1 - d2.astype(F32)).astype(BF16)
    dcat = jnp.concatenate([d1, d2, d3], axis=0)
    c3 = _dot(dcat, selc_ref[...])
    dt_all = jax.nn.softplus(c3[:tb] + c3[tb:2 * tb] + c3[2 * tb:] + bias_ref[...])
    r3 = _dot_tb(selr_ref[...], dcat)
    dtt_all = jax.nn.softplus(r3[:, :tb] + r3[:, tb:2 * tb] + r3[:, 2 * tb:] + biast_ref[...])

    u_all = jnp.concatenate([xs_ref[...], b_ref[...], c_ref[...]], axis=1)
    ucat = jnp.concatenate([tail_ref[...], u_all], axis=0)
    tail_ref[...] = u_all[tb - TAIL:, :]

    for ci in range(n_chunks):
        r0 = ci * L
        shifted = _dot(shift, ucat[r0:r0 + L + TAIL, :])
        acc = cb + u_all[r0:r0 + L, :].astype(F32) * cw[SSD_CONV - 1:SSD_CONV, :]
        for k in range(SSD_CONV - 1):
            acc = acc + shifted[k * L:(k + 1) * L, :] * cw[k:k + 1, :]
        conv = _silu(acc)
        xc = conv[:, :W]
        bc = conv[:, W:W + NS].astype(BF16)
        cc = conv[:, W + NS:].astype(BF16)

        dt = dt_all[r0:r0 + L, :]
        da_hi, da_lo = _split_hi_lo(dt * a_row)
        acs2 = _dot(tri, jnp.concatenate([da_hi, da_lo], axis=1))
        acs = acs2[:, :HG] + acs2[:, HG:]
        dat_hi, dat_lo = _split_hi_lo(dtt_all[:, r0:r0 + L] * a_col)
        acst2 = _dot(jnp.concatenate([dat_hi, dat_lo], axis=0), trit)
        acst = acst2[:HG, :] + acst2[HG:, :]

        acs_last = acs[L - 1:L, :]
        dt_e = expand_heads(dt)
        eacs_e = expand_heads(jnp.exp(acs))
        wend_e = expand_heads(dt * jnp.exp(acs_last - acs))
        xdt_b = (xc * dt_e).astype(BF16)

        cbm = _dot_tb(cc, bc)
        state = state_ref[...]
        y_off = _dot(cc, state.astype(BF16)) * eacs_e
        ys = []
        for h in range(HG):
            seg = acs[:, h:h + 1] - acst[h:h + 1, :]
            m = jnp.where(causal, jnp.exp(seg), 0.0) * cbm
            ys.append(_dot(m.astype(BF16), xdt_b[:, h * P:(h + 1) * P]))
        y = jnp.concatenate(ys, axis=1) + y_off + xc * dsk_ref[...]
        state_ref[...] = state * eacs_e[L - 1:L, :] + _dot_ta(bc, (xc * wend_e).astype(BF16))

        yg = y * _silu(z_ref[r0:r0 + L, :].astype(F32))
        yn = yg * lax.rsqrt(jnp.mean(yg * yg, axis=-1, keepdims=True) + EPS) * nw_ref[...]
        y_ref[r0:r0 + L, :] = yn.astype(y_ref.dtype)


def ssd_scan(proj, dtraw, conv_w, conv_b, dt_bias, a_log, d_skip, norm_w, bsz, tlen, tb):
    n = bsz * tlen
    n_heads = dt_bias.shape[0]
    g = n_heads // SSD_HEADS_PER_GROUP
    di = g * SSD_GROUP_W
    W, NS, HG = SSD_GROUP_W, SSD_D_STATE, SSD_HEADS_PER_GROUP
    nt = tlen // tb
    lane = np.arange(LANES)
    sel = (lane[None, :, None] == (np.arange(g)[:, None, None] * HG + np.arange(HG)[None, None, :]))
    selc = jnp.asarray(sel, BF16)
    selr = jnp.asarray(sel.transpose(0, 2, 1), BF16)
    bias = dt_bias.reshape(g, 1, HG)
    biast = dt_bias.reshape(g, HG, 1)
    a = -jnp.exp(a_log.astype(F32))
    a_row = a.reshape(g, 1, HG)
    a_col = a.reshape(g, HG, 1)
    dsk = jnp.repeat(d_skip.astype(F32), SSD_HEAD_DIM).reshape(1, di)
    cb = conv_b.reshape(1, -1)
    xoff = di // W
    boff = 2 * di // NS
    coff = boff + g
    rowblk = lambda b, gi, t: b * nt + t
    kern = functools.partial(_ssd_kernel, n_chunks=tb // SSD_CHUNK)
    return pl.pallas_call(
        kern,
        grid=(bsz, g, nt),
        in_specs=[
            pl.BlockSpec((tb, W), lambda b, gi, t: (rowblk(b, gi, t), gi)),
            pl.BlockSpec((tb, W), lambda b, gi, t: (rowblk(b, gi, t), xoff + gi)),
            pl.BlockSpec((tb, NS), lambda b, gi, t: (rowblk(b, gi, t), boff + gi)),
            pl.BlockSpec((tb, NS), lambda b, gi, t: (rowblk(b, gi, t), coff + gi)),
            pl.BlockSpec((tb, LANES), lambda b, gi, t: (rowblk(b, gi, t), 0)),
            pl.BlockSpec((None, LANES, HG), lambda b, gi, t: (gi, 0, 0)),
            pl.BlockSpec((None, HG, LANES), lambda b, gi, t: (gi, 0, 0)),
            pl.BlockSpec((SSD_CONV, W), lambda b, gi, t: (0, gi)),
            pl.BlockSpec((SSD_CONV, NS), lambda b, gi, t: (0, di // NS + gi)),
            pl.BlockSpec((SSD_CONV, NS), lambda b, gi, t: (0, di // NS + g + gi)),
            pl.BlockSpec((1, W), lambda b, gi, t: (0, gi)),
            pl.BlockSpec((1, NS), lambda b, gi, t: (0, di // NS + gi)),
            pl.BlockSpec((1, NS), lambda b, gi, t: (0, di // NS + g + gi)),
            pl.BlockSpec((None, 1, HG), lambda b, gi, t: (gi, 0, 0)),
            pl.BlockSpec((None, HG, 1), lambda b, gi, t: (gi, 0, 0)),
            pl.BlockSpec((None, 1, HG), lambda b, gi, t: (gi, 0, 0)),
            pl.BlockSpec((None, HG, 1), lambda b, gi, t: (gi, 0, 0)),
            pl.BlockSpec((1, W), lambda b, gi, t: (0, gi)),
            pl.BlockSpec((1, W), lambda b, gi, t: (0, gi)),
        ],
        out_specs=pl.BlockSpec((tb, W), lambda b, gi, t: (rowblk(b, gi, t), gi)),
        out_shape=jax.ShapeDtypeStruct((n, di), BF16),
        scratch_shapes=[
            pltpu.VMEM((NS, W), F32),
            pltpu.VMEM((BF16_ROWS, W + 2 * NS), BF16),
        ],
        compiler_params=_cparams(("parallel", "parallel", "arbitrary")),
        name="ssd_scan",
    )(proj, proj, proj, proj, dtraw, selc, selr, conv_w, conv_w, conv_w, cb, cb, cb,
      bias, biast, a_row, a_col, dsk, norm_w.reshape(1, di))


def _gla_tables():
    c = GLA_CHUNK
    r = np.arange(c)
    d = np.zeros((GLA_LEVELS + 2, c, c), np.float32)
    masks = np.zeros((GLA_LEVELS + 1, c, c), np.float32)
    upper = np.zeros((GLA_LEVELS, c, 1), np.float32)
    masks[0] = np.eye(c)
    for j in range(1, GLA_LEVELS + 1):
        half = 1 << (j - 1)
        blk = r >> j
        mid = (blk << j) + half
        up = (r & ((1 << j) - 1)) >= half
        upper[j - 1, :, 0] = up
        for l in range(c):
            if up[l]:
                d[j - 1, l, mid[l]:l + 1] = 1.0
            else:
                d[j - 1, l, l + 1:mid[l]] = 1.0
        same = blk[:, None] == blk[None, :]
        masks[j] = same & up[:, None] & (~up)[None, :]
    d[GLA_LEVELS] = np.tril(np.ones((c, c)))
    d[GLA_LEVELS + 1] = np.triu(np.ones((c, c)), 1)
    return d.reshape(-1, c), masks, upper


def _gla_kernel(q_ref, k_ref, v_ref, g_ref, alr_ref, wa_ref, ba_ref, nw_ref,
                dall_ref, mask_ref, up_ref, o_ref, state_ref, *, n_chunks, hk):
    C = GLA_CHUNK
    scale = hk ** -0.5

    @pl.when(pl.program_id(2) == 0)
    def _():
        state_ref[...] = jnp.zeros_like(state_ref)

    wa = wa_ref[...]
    dall = dall_ref[...]

    a_hi, a_lo = _split_hi_lo(alr_ref[...])
    s = _dot(a_hi, wa)
    pre = s[:, :hk] + s[:, hk:] + _dot(a_lo, wa[:, :hk]) + ba_ref[...]
    lg = (jnp.minimum(pre, 0.0) - jnp.log1p(jnp.exp(-jnp.abs(pre)))) * (1.0 / GLA_GATE_TAU)
    lg_hi, lg_lo = _split_hi_lo(lg)
    lg2 = jnp.concatenate([lg_hi, lg_lo], axis=1)

    for ci in range(n_chunks):
        r0 = ci * C
        e2 = _dot(dall, lg2[r0:r0 + C, :])
        f = jnp.exp(e2[:, :hk] + e2[:, hk:])

        q = q_ref[r0:r0 + C, :].astype(F32) * scale
        k = k_ref[r0:r0 + C, :].astype(F32)
        v = v_ref[r0:r0 + C, :]
        sc = _dot_tb(q.astype(BF16), k.astype(BF16)) * mask_ref[0]
        for j in range(GLA_LEVELS):
            fj = f[j * C:(j + 1) * C, :]
            up = up_ref[j] > 0.5
            qj = jnp.where(up, q * fj, 0.0).astype(BF16)
            kj = jnp.where(up, 0.0, k * fj).astype(BF16)
            sc = sc + _dot_tb(qj, kj) * mask_ref[j + 1]
        f_cum = f[GLA_LEVELS * C:(GLA_LEVELS + 1) * C, :]
        f_end = f[(GLA_LEVELS + 1) * C:, :]
        state = state_ref[...]
        o = _dot(sc.astype(BF16), v) + _dot_tb((q * f_cum).astype(BF16), state.astype(BF16))
        state_ref[...] = state * f_cum[C - 1:C, :] + _dot_ta(v, (k * f_end).astype(BF16))

        on = o * lax.rsqrt(jnp.mean(o * o, axis=-1, keepdims=True) + EPS) * nw_ref[...]
        o_ref[r0:r0 + C, :] = (on * _silu(g_ref[r0:r0 + C, :].astype(F32))).astype(o_ref.dtype)


def gla_scan(proj, alr, w_a2, b_a, norm_w, bsz, tlen, tb):
    n = bsz * tlen
    dk = w_a2.shape[1]
    hk = dk // GLA_N_HEADS
    dv = (proj.shape[1] - 2 * dk) // 2
    hv = dv // GLA_N_HEADS
    nt = tlen // tb
    H = GLA_N_HEADS
    wa = jnp.zeros((LANES, dk), F32).at[:w_a2.shape[0], :].set(w_a2)
    wa_hi, wa_lo = _split_hi_lo(wa)
    wa_cat = jnp.concatenate([wa_hi.reshape(LANES, H, hk), wa_lo.reshape(LANES, H, hk)], axis=2)
    wa_cat = wa_cat.reshape(LANES, 2 * dk)
    d_all, masks, upper = _gla_tables()
    d_all = jnp.asarray(d_all, BF16)
    masks = jnp.asarray(masks, F32)
    upper = jnp.asarray(upper, F32)
    rowblk = lambda b, h, t: b * nt + t
    kern = functools.partial(_gla_kernel, n_chunks=tb // GLA_CHUNK, hk=hk)
    return pl.pallas_call(
        kern,
        grid=(bsz, H, nt),
        in_specs=[
            pl.BlockSpec((tb, hk), lambda b, h, t: (rowblk(b, h, t), h)),
            pl.BlockSpec((tb, hk), lambda b, h, t: (rowblk(b, h, t), H + h)),
            pl.BlockSpec((tb, hv), lambda b, h, t: (rowblk(b, h, t), 2 * dk // hv + h)),
            pl.BlockSpec((tb, hv), lambda b, h, t: (rowblk(b, h, t), 2 * dk // hv + H + h)),
            pl.BlockSpec((tb, LANES), lambda b, h, t: (rowblk(b, h, t), 0)),
            pl.BlockSpec((LANES, 2 * hk), lambda b, h, t: (0, h)),
            pl.BlockSpec((1, hk), lambda b, h, t: (0, h)),
            pl.BlockSpec((1, hv), lambda b, h, t: (0, 0)),
            pl.BlockSpec(d_all.shape, lambda b, h, t: (0, 0)),
            pl.BlockSpec(masks.shape, lambda b, h, t: (0, 0, 0)),
            pl.BlockSpec(upper.shape, lambda b, h, t: (0, 0, 0)),
        ],
        out_specs=pl.BlockSpec((tb, hv), lambda b, h, t: (rowblk(b, h, t), h)),
        out_shape=jax.ShapeDtypeStruct((n, dv), BF16),
        scratch_shapes=[pltpu.VMEM((hv, hk), F32)],
        compiler_params=_cparams(("parallel", "parallel", "arbitrary")),
        name="gla_scan",
    )(proj, proj, proj, proj, alr, wa_cat, b_a.reshape(1, dk), norm_w.reshape(1, hv),
      d_all, masks, upper)


def _route_kernel(lg_ref, bias_ref, ids_ref, gates_ref, cnt_ref, base_ref):
    tm = lg_ref.shape[0]
    G, EPG, E = MOE_GROUPS, MOE_EXPERTS_PER_GROUP, MOE_N_EXPERTS

    @pl.when(pl.program_id(0) == 0)
    def _():
        base_ref[...] = jnp.zeros_like(base_ref)

    lane = lax.broadcasted_iota(I32, (tm, LANES), 1)
    logits = lg_ref[...] + bias_ref[...]
    neg = jnp.float32(-jnp.inf)
    big = jnp.int32(LANES)
    is_g = lane < G
    lg_g = jnp.where(is_g, logits, neg)
    m_g = jnp.max(lg_g, axis=-1, keepdims=True)
    gi = jnp.min(jnp.where(lg_g == m_g, lane, big), axis=-1, keepdims=True)
    z_g = jnp.sum(jnp.where(is_g, jnp.exp(lg_g - m_g), 0.0), axis=-1, keepdims=True)
    grp_w = 1.0 / z_g
    in_grp = (lane >= G) & (lane < G + E) & (((lane - G) // EPG) == gi)
    lg_e = jnp.where(in_grp, logits, neg)
    m1 = jnp.max(lg_e, axis=-1, keepdims=True)
    l1 = jnp.min(jnp.where(lg_e == m1, lane, big), axis=-1, keepdims=True)
    lg_e2 = jnp.where(lane == l1, neg, lg_e)
    m2 = jnp.max(lg_e2, axis=-1, keepdims=True)
    l2 = jnp.min(jnp.where(lg_e2 == m2, lane, big), axis=-1, keepdims=True)
    p2 = jnp.exp(m2 - m1)
    g1 = grp_w / (1.0 + p2)
    g2 = grp_w * p2 / (1.0 + p2)
    e1 = l1 - G
    e2 = l2 - G
    onehot = jnp.where((lane == e1) | (lane == e2), 1.0, 0.0)
    r = lax.broadcasted_iota(I32, (tm, tm), 0)
    c = lax.broadcasted_iota(I32, (tm, tm), 1)
    strict = jnp.where(c < r, 1.0, 0.0).astype(BF16)
    cum = _dot(strict, onehot.astype(BF16)) + base_ref[0:1, :]
    rank1 = jnp.sum(jnp.where(lane == e1, cum, 0.0), axis=-1, keepdims=True).astype(I32)
    rank2 = jnp.sum(jnp.where(lane == e2, cum, 0.0), axis=-1, keepdims=True).astype(I32)
    total = base_ref[0:1, :] + jnp.sum(onehot, axis=0, keepdims=True)
    base_ref[...] = jnp.broadcast_to(total, base_ref.shape)
    cnt_ref[...] = jnp.broadcast_to(total, cnt_ref.shape).astype(I32)

    ids = jnp.where(lane == 0, e1, jnp.where(lane == 1, e2, jnp.where(lane == 2, rank1,
                    jnp.where(lane == 3, rank2, 0))))
    ids_ref[...] = ids
    gates_ref[...] = jnp.where(lane == 0, g1, jnp.where(lane == 1, g2, 0.0))


def moe_route(logits, bias, tm):
    n = logits.shape[0]
    return pl.pallas_call(
        _route_kernel,
        grid=(n // tm,),
        in_specs=[
            pl.BlockSpec((tm, LANES), lambda i: (i, 0)),
            pl.BlockSpec((1, LANES), lambda i: (0, 0)),
        ],
        out_specs=[
            pl.BlockSpec((tm, LANES), lambda i: (i, 0)),
            pl.BlockSpec((tm, LANES), lambda i: (i, 0)),
            pl.BlockSpec((8, LANES), lambda i: (0, 0)),
        ],
        out_shape=[
            jax.ShapeDtypeStruct((n, LANES), I32),
            jax.ShapeDtypeStruct((n, LANES), F32),
            jax.ShapeDtypeStruct((8, LANES), I32),
        ],
        scratch_shapes=[pltpu.VMEM((8, LANES), F32)],
        compiler_params=_cparams(("arbitrary",)),
        name="moe_route",
    )(logits, bias)


def _dispatch_kernel(zflag_ref, dest_ref, h_ref, xb_ref, zbuf_ref, sem, zsem, *, ns):
    tm = h_ref.shape[0] // ns
    blk = MOE_ROWS * ns
    n_blocks = xb_ref.shape[0] // blk

    @pl.when(pl.program_id(0) == 0)
    def _():
        zbuf_ref[...] = jnp.zeros_like(zbuf_ref)

        def zero_copy(b):
            r0 = pl.multiple_of(b * blk, blk)
            return pltpu.make_async_copy(zbuf_ref, xb_ref.at[pl.ds(r0, blk)], zsem)

        def zstart(b, carry):
            @pl.when(zflag_ref[b] > 0)
            def _():
                zero_copy(b).start()
            return carry

        def zwait(b, carry):
            @pl.when(zflag_ref[b] > 0)
            def _():
                zero_copy(b).wait()
            return carry

        lax.fori_loop(0, n_blocks, zstart, 0)
        lax.fori_loop(0, n_blocks, zwait, 0)

    def issue(r, carry):
        src = h_ref.at[pl.ds(pl.multiple_of(r * ns, ns), ns)]
        for s in range(MOE_TOP_K):
            d = pl.multiple_of(dest_ref[0, 0, MOE_TOP_K * r + s], ns)
            pltpu.make_async_copy(src, xb_ref.at[pl.ds(d, ns)], sem).start(priority=s % 2)
        return carry

    lax.fori_loop(0, tm, issue, 0, unroll=DMA_UNROLL)

    def drain(r, carry):
        for s in range(MOE_TOP_K):
            pltpu.make_async_copy(h_ref.at[pl.ds(0, ns)], xb_ref.at[pl.ds(0, ns)], sem).wait()
        return carry

    lax.fori_loop(0, tm, drain, 0, unroll=DMA_UNROLL)


def moe_dispatch(hp, dest, zflag, n_rows, ns, tm):
    n = hp.shape[0] // ns
    dest3 = dest.reshape(n // tm, 1, MOE_TOP_K * tm)
    return pl.pallas_call(
        functools.partial(_dispatch_kernel, ns=ns),
        grid=(n // tm,),
        in_specs=[
            pl.BlockSpec(memory_space=pltpu.SMEM),
            pl.BlockSpec((1, 1, MOE_TOP_K * tm), lambda i: (i, 0, 0), memory_space=pltpu.SMEM),
            pl.BlockSpec((tm * ns, LANES), lambda i: (i, 0)),
        ],
        out_specs=pl.BlockSpec(memory_space=pl.ANY),
        out_shape=jax.ShapeDtypeStruct((n_rows * ns, LANES), U32),
        scratch_shapes=[pltpu.VMEM((MOE_ROWS * ns, LANES), U32), pltpu.SemaphoreType.DMA(()),
                        pltpu.SemaphoreType.DMA(())],
        compiler_params=_cparams(("arbitrary",)),
        name="moe_dispatch",
    )(zflag, dest3, hp)


def _experts_kernel(be_ref, nb_ref, x_ref, wg_ref, wu_ref, wd_ref, y_ref, wgu_s, wd_s):
    i = pl.program_id(0)
    ff = wg_ref.shape[1]
    active = i < nb_ref[0]
    new_expert = (i == 0) | (be_ref[i] != be_ref[jnp.maximum(i - 1, 0)])

    @pl.when(active & new_expert)
    def _():
        wgu_s[:, :ff] = wg_ref[...].astype(BF16)
        wgu_s[:, ff:] = wu_ref[...].astype(BF16)
        wd_s[...] = wd_ref[...].astype(BF16)

    @pl.when(active)
    def _():
        xa, xb = _unpack_bf16_pair(_load_rows(x_ref, x_ref.shape[0] // MOE_ROWS))
        half = xa.shape[1]
        h = _dot(xa.astype(BF16), wgu_s[:half, :]) + _dot(xb.astype(BF16), wgu_s[half:, :])
        act = (_silu(h[:, :ff]) * h[:, ff:]).astype(BF16)
        y = _dot(act, wd_s[...])
        _store_rows(y_ref, _pack_bf16_pair(y[:, :half], y[:, half:]))

    @pl.when(jnp.logical_not(active))
    def _():
        y_ref[...] = jnp.zeros_like(y_ref)


def moe_experts(xbuf, block_e, n_used, w_gate, w_up, w_down, layer):
    d, ff = w_gate.shape[2], w_gate.shape[3]
    ns = d // 2 // LANES
    blk = MOE_ROWS * ns
    nb = xbuf.shape[0] // blk
    xmap = lambda i, be, nu: (jnp.minimum(i, nu[0] - 1), 0)
    wmap = lambda i, be, nu: (layer, be[i], 0, 0)
    grid_spec = pltpu.PrefetchScalarGridSpec(
        num_scalar_prefetch=2,
        grid=(nb,),
        in_specs=[
            pl.BlockSpec((blk, LANES), xmap),
            pl.BlockSpec((None, None, d, ff), wmap),
            pl.BlockSpec((None, None, d, ff), wmap),
            pl.BlockSpec((None, None, ff, d), wmap),
        ],
        out_specs=pl.BlockSpec((blk, LANES), lambda i, be, nu: (i, 0)),
        scratch_shapes=[pltpu.VMEM((d, 2 * ff), BF16), pltpu.VMEM((ff, d), BF16)],
    )
    return pl.pallas_call(
        _experts_kernel,
        grid_spec=grid_spec,
        out_shape=jax.ShapeDtypeStruct(xbuf.shape, U32),
        compiler_params=_cparams(("arbitrary",)),
        name="moe_experts",
    )(block_e, n_used, xbuf, w_gate, w_up, w_down)


def _combine_kernel(dest_ref, x_ref, gates_ref, nw_ref, yb_ref, o_ref, buf_ref, sem, *, final_norm):
    tm = x_ref.shape[0]
    ns = buf_ref.shape[1] // tm

    def issue(r, carry):
        r0 = pl.multiple_of(r * ns, ns)
        for s in range(MOE_TOP_K):
            d = pl.multiple_of(dest_ref[0, 0, MOE_TOP_K * r + s], ns)
            pltpu.make_async_copy(yb_ref.at[pl.ds(d, ns)], buf_ref.at[s, pl.ds(r0, ns)], sem).start(priority=s % 2)
        return carry

    lax.fori_loop(0, tm, issue, 0, unroll=DMA_UNROLL)

    def drain(r, carry):
        for s in range(MOE_TOP_K):
            pltpu.make_async_copy(yb_ref.at[pl.ds(0, ns)], buf_ref.at[0, pl.ds(0, ns)], sem).wait()
        return carry

    lax.fori_loop(0, tm, drain, 0, unroll=DMA_UNROLL)

    gates = gates_ref[...]
    a1, b1 = _unpack_bf16_pair(_load_rows(buf_ref.at[0], ns))
    a2, b2 = _unpack_bf16_pair(_load_rows(buf_ref.at[1], ns))
    g1 = gates[:, 0:1]
    g2 = gates[:, 1:2]
    x = x_ref[...]
    half = a1.shape[1]
    xn = jnp.concatenate([x[:, :half] + g1 * a1 + g2 * a2, x[:, half:] + g1 * b1 + g2 * b2], axis=1)
    if final_norm:
        xn = xn * lax.rsqrt(jnp.mean(xn * xn, axis=-1, keepdims=True) + EPS) * nw_ref[...]
    o_ref[...] = xn


def moe_combine(x, ybuf, dest, gates, nw, tm, final_norm):
    n, d = x.shape
    ns = d // 2 // LANES
    dest3 = dest.reshape(n // tm, 1, MOE_TOP_K * tm)
    kern = functools.partial(_combine_kernel, final_norm=final_norm)
    return pl.pallas_call(
        kern,
        grid=(n // tm,),
        in_specs=[
            pl.BlockSpec((1, 1, MOE_TOP_K * tm), lambda i: (i, 0, 0), memory_space=pltpu.SMEM),
            pl.BlockSpec((tm, d), lambda i: (i, 0)),
            pl.BlockSpec((tm, LANES), lambda i: (i, 0)),
            pl.BlockSpec((1, d), lambda i: (0, 0)),
            pl.BlockSpec(memory_space=pl.ANY),
        ],
        out_specs=pl.BlockSpec((tm, d), lambda i: (i, 0)),
        out_shape=jax.ShapeDtypeStruct((n, d), F32),
        scratch_shapes=[pltpu.VMEM((MOE_TOP_K, tm * ns, LANES), U32), pltpu.SemaphoreType.DMA(())],
        compiler_params=_cparams(("arbitrary",)),
        name="moe_combine",
    )(dest3, x, gates, nw.reshape(1, d), ybuf)


def hier_moe(x, hp, logits, b_group, b_expert, w_gate, w_up, w_down, layer, nw_final, final_norm, tiles):
    n = x.shape[0]
    E = MOE_N_EXPERTS
    bias = jnp.zeros((1, LANES), F32).at[0, :MOE_GROUPS].set(b_group).at[0, MOE_GROUPS:MOE_GROUPS + E].set(b_expert)
    ids, gates, cnt = moe_route(logits, bias, tiles["route"])
    counts = cnt[0, :E]
    pcounts = (counts + MOE_ROWS - 1) // MOE_ROWS * MOE_ROWS
    pends = jnp.cumsum(pcounts)
    poffs = pends - pcounts
    n_blocks = (n * MOE_TOP_K) // MOE_ROWS + E
    eidx = jnp.arange(E, dtype=I32)
    sel_off = jnp.sum(jnp.where(ids[:, 0:MOE_TOP_K, None] == eidx, poffs, 0), axis=-1)
    ns = x.shape[1] // 2 // LANES
    dest = (sel_off + ids[:, MOE_TOP_K:2 * MOE_TOP_K]) * ns
    blk_start = jnp.arange(n_blocks, dtype=I32) * MOE_ROWS
    n_used = (pends[-1] // MOE_ROWS).astype(I32)
    block_e = jnp.minimum(jnp.sum((pends[None, :] <= blk_start[:, None]).astype(I32), axis=1), E - 1)
    last_e = jnp.sum(jnp.where(jnp.arange(n_blocks) == n_used - 1, block_e, 0))
    block_e = jnp.where(jnp.arange(n_blocks) < n_used, block_e, last_e).astype(I32)
    blk_end = blk_start + MOE_ROWS
    is_last = jnp.any((pends[None, :] == blk_end[:, None]) & (counts[None, :] > 0), axis=1)
    zflag = (is_last | (jnp.arange(n_blocks) >= n_used)).astype(I32)
    xbuf = moe_dispatch(hp, dest, zflag, n_blocks * MOE_ROWS, ns, tiles["dispatch"])
    ybuf = moe_experts(xbuf, block_e, n_used.reshape(1), w_gate, w_up, w_down, layer)
    return moe_combine(x, ybuf, dest, gates, nw_final, tiles["combine"], final_norm)


def _tiles(n, tlen):
    pick = lambda pref, total: next(t for t in (pref, 512, 256, 128, 64) if t <= pref and total % t == 0)
    return {
        "mm_m": pick(1024, n), "mm_n": 1024, "out_m": pick(512, n),
        "ssd_t": pick(512, tlen), "gla_t": pick(512, tlen),
        "route": pick(512, n), "dispatch": pick(256, n), "combine": pick(256, n),
    }


def kernel(x, ssd_w_in, ssd_conv_w, ssd_conv_b, ssd_dt_bias, ssd_a_log, ssd_d, ssd_norm_w, ssd_w_out,
           gla_w_in, gla_w_a2, gla_b_a, gla_norm_w, gla_w_out, norm_mix, norm_ffn, moe_w_group,
           moe_b_group, moe_w_expert, moe_b_expert, moe_w_gate, moe_w_up, moe_w_down, norm_final):
    bsz, tlen, d = x.shape
    n = bsz * tlen
    depth = norm_mix.shape[0]
    tiles = _tiles(n, tlen)
    xf = x.reshape(n, d)
    ssd_w_in_b, ssd_w_out_b = ssd_w_in.astype(BF16), ssd_w_out.astype(BF16)
    gla_w_in_b, gla_w_out_b = gla_w_in.astype(BF16), gla_w_out.astype(BF16)
    for i in range(depth):
        j = i // 2
        if i % 2 == 0:
            m_main = ssd_w_in.shape[2] - ssd_dt_bias.shape[1]
            proj, dtraw = norm_matmul(xf, norm_mix[i], ssd_w_in_b, j, m_main,
                                      _small_weight(ssd_w_in[j, :, m_main:]), tiles["mm_m"], tiles["mm_n"])
            y = ssd_scan(proj, dtraw, ssd_conv_w[j], ssd_conv_b[j], ssd_dt_bias[j], ssd_a_log[j],
                         ssd_d[j], ssd_norm_w[j], bsz, tlen, tiles["ssd_t"])
            w_out = ssd_w_out_b
        else:
            m_main = gla_w_in.shape[2] - GLA_GATE_RANK
            proj, alr = norm_matmul(xf, norm_mix[i], gla_w_in_b, j, m_main,
                                    _small_weight(gla_w_in[j, :, m_main:]), tiles["mm_m"], tiles["mm_n"])
            y = gla_scan(proj, alr, gla_w_a2[j], gla_b_a[j], gla_norm_w[j], bsz, tlen, tiles["gla_t"])
            w_out = gla_w_out_b
        w_router = _small_weight(jnp.concatenate([moe_w_group[i], moe_w_expert[i]], axis=1))
        xf, hp, logits = out_proj(y, w_out, j, xf, norm_ffn[i], w_router, tiles["out_m"])
        xf = hier_moe(xf, hp, logits, moe_b_group[i], moe_b_expert[i], moe_w_gate, moe_w_up,
                      moe_w_down, i, norm_final, i == depth - 1, tiles)
    return xf.reshape(bsz, tlen, d)
```

```python
import functools

import numpy as np
import jax
import jax.numpy as jnp
from jax import lax
from jax.experimental import pallas as pl
from jax.experimental.pallas import tpu as pltpu

F32 = jnp.float32
BF16 = jnp.bfloat16
U32 = jnp.uint32
I32 = jnp.int32

EPS = 1e-6
LANES = 128
BF16_ROWS = 16
VMEM_LIMIT = 56 * 1024 * 1024

SSD_HEAD_DIM = 64
SSD_HEADS_PER_GROUP = 8
SSD_D_STATE = 128
SSD_CONV = 4
SSD_CHUNK = 128
SSD_GROUP_W = SSD_HEAD_DIM * SSD_HEADS_PER_GROUP

GLA_N_HEADS = 4
GLA_GATE_RANK = 16
GLA_GATE_TAU = 16.0
GLA_CHUNK = 128
GLA_LEVELS = 7

MOE_GROUPS = 8
MOE_EXPERTS_PER_GROUP = 8
MOE_N_EXPERTS = 64
MOE_TOP_K = 2
MOE_ROWS = 512
DMA_UNROLL = 8


def _cparams(sem):
    return pltpu.CompilerParams(dimension_semantics=sem, vmem_limit_bytes=VMEM_LIMIT)


def _split_hi_lo(v):
    hi = v.astype(BF16)
    lo = (v - hi.astype(F32)).astype(BF16)
    return hi, lo


def _dot(a, b):
    return jnp.dot(a, b, preferred_element_type=F32)


def _dot_tb(a, b):
    return lax.dot_general(a, b, (((1,), (1,)), ((), ())), preferred_element_type=F32)


def _dot_ta(a, b):
    return lax.dot_general(a, b, (((0,), (0,)), ((), ())), preferred_element_type=F32)


def _silu(v):
    hv = 0.5 * v
    return hv + hv * jnp.tanh(hv)


def _pack_bf16_pair(a, b):
    ua = lax.bitcast_convert_type(a.astype(BF16).astype(F32), U32)
    ub = lax.bitcast_convert_type(b.astype(BF16).astype(F32), U32)
    return ua | (ub >> 16)


def _unpack_bf16_pair(u):
    a = lax.bitcast_convert_type(u & jnp.uint32(0xFFFF0000), F32)
    b = lax.bitcast_convert_type(u << 16, F32)
    return a, b


def _store_rows(ref, v):
    ns = v.shape[1] // LANES
    for s in range(ns):
        ref[pl.ds(s, v.shape[0], stride=ns), :] = v[:, s * LANES:(s + 1) * LANES]


def _load_rows(ref, ns):
    rows = ref.shape[0] // ns
    return jnp.concatenate([ref[pl.ds(s, rows, stride=ns), :] for s in range(ns)], axis=1)


def _norm_matmul_kernel(x_ref, nw_ref, w_ref, ws_ref, o_ref, os_ref, h_ref):
    j = pl.program_id(1)

    @pl.when(j == 0)
    def _():
        x = x_ref[...]
        h = x * lax.rsqrt(jnp.mean(x * x, axis=-1, keepdims=True) + EPS) * nw_ref[...]
        hi, lo = _split_hi_lo(h)
        h_ref[...] = hi
        s = _dot(hi, ws_ref[...])
        os_ref[...] = s[:, :LANES] + s[:, LANES:] + _dot(lo, ws_ref[:, :LANES])

    o_ref[...] = _dot(h_ref[...], w_ref[...]).astype(o_ref.dtype)


def _small_weight(w):
    k, n = w.shape
    wp = jnp.zeros((k, LANES), F32).at[:, :n].set(w)
    hi, lo = _split_hi_lo(wp)
    return jnp.concatenate([hi, lo], axis=1)


def norm_matmul(x, nw, w_main, layer, m, w_small, tm, tn):
    n, d = x.shape
    return pl.pallas_call(
        _norm_matmul_kernel,
        grid=(n // tm, m // tn),
        in_specs=[
            pl.BlockSpec((tm, d), lambda i, j: (i, 0)),
            pl.BlockSpec((1, d), lambda i, j: (0, 0)),
            pl.BlockSpec((None, d, tn), lambda i, j: (layer, 0, j)),
            pl.BlockSpec((d, 2 * LANES), lambda i, j: (0, 0)),
        ],
        out_specs=[
            pl.BlockSpec((tm, tn), lambda i, j: (i, j)),
            pl.BlockSpec((tm, LANES), lambda i, j: (i, 0)),
        ],
        out_shape=[
            jax.ShapeDtypeStruct((n, m), BF16),
            jax.ShapeDtypeStruct((n, LANES), F32),
        ],
        scratch_shapes=[pltpu.VMEM((tm, d), BF16)],
        compiler_params=_cparams(("parallel", "arbitrary")),
        name="norm_matmul",
    )(x, nw.reshape(1, d), w_main, w_small)


def _out_proj_kernel(y_ref, w_ref, x_ref, nw_ref, wr_ref, xo_ref, hp_ref, lg_ref):
    xn = x_ref[...] + _dot(y_ref[...], w_ref[...])
    xo_ref[...] = xn
    h = xn * lax.rsqrt(jnp.mean(xn * xn, axis=-1, keepdims=True) + EPS) * nw_ref[...]
    half = h.shape[1] // 2
    _store_rows(hp_ref, _pack_bf16_pair(h[:, :half], h[:, half:]))
    hi, lo = _split_hi_lo(h)
    s = _dot(hi, wr_ref[...])
    lg_ref[...] = s[:, :LANES] + s[:, LANES:] + _dot(lo, wr_ref[:, :LANES])


def out_proj(y, w_out, layer, x, nw, w_router, tm):
    n, k = y.shape
    d = x.shape[1]
    return pl.pallas_call(
        _out_proj_kernel,
        grid=(n // tm,),
        in_specs=[
            pl.BlockSpec((tm, k), lambda i: (i, 0)),
            pl.BlockSpec((None, k, d), lambda i: (layer, 0, 0), pipeline_mode=pl.Buffered(1)),
            pl.BlockSpec((tm, d), lambda i: (i, 0)),
            pl.BlockSpec((1, d), lambda i: (0, 0)),
            pl.BlockSpec((d, 2 * LANES), lambda i: (0, 0)),
        ],
        out_specs=[
            pl.BlockSpec((tm, d), lambda i: (i, 0)),
            pl.BlockSpec((tm * (d // 2 // LANES), LANES), lambda i: (i, 0)),
            pl.BlockSpec((tm, LANES), lambda i: (i, 0)),
        ],
        out_shape=[
            jax.ShapeDtypeStruct((n, d), F32),
            jax.ShapeDtypeStruct((n * (d // 2 // LANES), LANES), U32),
            jax.ShapeDtypeStruct((n, LANES), F32),
        ],
        compiler_params=_cparams(("parallel",)),
        name="out_proj",
    )(y, w_out, x, nw.reshape(1, d), w_router)


def _ssd_kernel(z_ref, xs_ref, b_ref, c_ref, dt_ref, selc_ref, selr_ref,
                cwx_ref, cwb_ref, cwc_ref, cbx_ref, cbb_ref, cbc_ref,
                bias_ref, biast_ref, a_ref, at_ref, dsk_ref, nw_ref,
                y_ref, state_ref, tail_ref, *, n_chunks):
    L = SSD_CHUNK
    W = SSD_GROUP_W
    NS = SSD_D_STATE
    HG = SSD_HEADS_PER_GROUP
    P = SSD_HEAD_DIM
    TAIL = BF16_ROWS
    WC = W + 2 * NS
    tb = n_chunks * L

    @pl.when(pl.program_id(2) == 0)
    def _():
        state_ref[...] = jnp.zeros_like(state_ref)
        tail_ref[...] = jnp.zeros_like(tail_ref)

    row = lax.broadcasted_iota(I32, (L, L), 0)
    col = lax.broadcasted_iota(I32, (L, L), 1)
    causal = row >= col
    tri = jnp.where(causal, 1.0, 0.0).astype(BF16)
    trit = jnp.where(col >= row, 1.0, 0.0).astype(BF16)
    er = lax.broadcasted_iota(I32, (2 * HG, W), 0)
    ec = lax.broadcasted_iota(I32, (2 * HG, W), 1)
    expand = jnp.where((er % HG) == (ec // P), 1.0, 0.0).astype(BF16)
    sr = lax.broadcasted_iota(I32, ((SSD_CONV - 1) * L, L + TAIL), 0)
    sc = lax.broadcasted_iota(I32, ((SSD_CONV - 1) * L, L + TAIL), 1)
    shift = jnp.where(sc == (sr % L) + (sr // L) + (TAIL - SSD_CONV + 1), 1.0, 0.0).astype(BF16)

    cw = jnp.concatenate([cwx_ref[...], cwb_ref[...], cwc_ref[...]], axis=1)
    cb = jnp.concatenate([cbx_ref[...], cbb_ref[...], cbc_ref[...]], axis=1)
    a_row = a_ref[...]
    a_col = at_ref[...]

    def expand_heads(v):
        hi, lo = _split_hi_lo(v)
        return _dot(jnp.concatenate([hi, lo], axis=1), expand)

    d = dt_ref[...]
    d1 = d.astype(BF16)
    r1 = d - d1.astype(F32)
    d2 = r1.astype(BF16)
    d3 = (r1 - d2.astype(F32)).astype(BF16)
    dcat = jnp.concatenate([d1, d2, d3], axis=0)
    c3 = _dot(dcat, selc_ref[...])
    dt_all = jax.nn.softplus(c3[:tb] + c3[tb:2 * tb] + c3[2 * tb:] + bias_ref[...])
    r3 = _dot_tb(selr_ref[...], dcat)
    dtt_all = jax.nn.softplus(r3[:, :tb] + r3[:, tb:2 * tb] + r3[:, 2 * tb:] + biast_ref[...])

    u_all = jnp.concatenate([xs_ref[...], b_ref[...], c_ref[...]], axis=1)
    ucat = jnp.concatenate([tail_ref[...], u_all], axis=0)
    tail_ref[...] = u_all[tb - TAIL:, :]

    for ci in range(n_chunks):
        r0 = ci * L
        shifted = _dot(shift, ucat[r0:r0 + L + TAIL, :])
        acc = cb + u_all[r0:r0 + L, :].astype(F32) * cw[SSD_CONV - 1:SSD_CONV, :]
        for k in range(SSD_CONV - 1):
            acc = acc + shifted[k * L:(k + 1) * L, :] * cw[k:k + 1, :]
        conv = _silu(acc)
        xc = conv[:, :W]
        bc = conv[:, W:W + NS].astype(BF16)
        cc = conv[:, W + NS:].astype(BF16)

        dt = dt_all[r0:r0 + L, :]
        da_hi, da_lo = _split_hi_lo(dt * a_row)
        acs2 = _dot(tri, jnp.concatenate([da_hi, da_lo], axis=1))
        acs = acs2[:, :HG] + acs2[:, HG:]
        dat_hi, dat_lo = _split_hi_lo(dtt_all[:, r0:r0 + L] * a_col)
        acst2 = _dot(jnp.concatenate([dat_hi, dat_lo], axis=0), trit)
        acst = acst2[:HG, :] + acst2[HG:, :]

        acs_last = acs[L - 1:L, :]
        dt_e = expand_heads(dt)
        eacs_e = expand_heads(jnp.exp(acs))
        wend_e = expand_heads(dt * jnp.exp(acs_last - acs))
        xdt_b = (xc * dt_e).astype(BF16)

        cbm = _dot_tb(cc, bc)
        state = state_ref[...]
        y_off = _dot(cc, state.astype(BF16)) * eacs_e
        ys = []
        for h in range(HG):
            seg = acs[:, h:h + 1] - acst[h:h + 1, :]
            m = jnp.where(causal, jnp.exp(seg), 0.0) * cbm
            ys.append(_dot(m.astype(BF16), xdt_b[:, h * P:(h + 1) * P]))
        y = jnp.concatenate(ys, axis=1) + y_off + xc * dsk_ref[...]
        state_ref[...] = state * eacs_e[L - 1:L, :] + _dot_ta(bc, (xc * wend_e).astype(BF16))

        yg = y * _silu(z_ref[r0:r0 + L, :].astype(F32))
        yn = yg * lax.rsqrt(jnp.mean(yg * yg, axis=-1, keepdims=True) + EPS) * nw_ref[...]
        y_ref[r0:r0 + L, :] = yn.astype(y_ref.dtype)


def ssd_scan(proj, dtraw, conv_w, conv_b, dt_bias, a_log, d_skip, norm_w, bsz, tlen, tb):
    n = bsz * tlen
    n_heads = dt_bias.shape[0]
    g = n_heads // SSD_HEADS_PER_GROUP
    di = g * SSD_GROUP_W
    W, NS, HG = SSD_GROUP_W, SSD_D_STATE, SSD_HEADS_PER_GROUP
    nt = tlen // tb
    lane = np.arange(LANES)
    sel = (lane[None, :, None] == (np.arange(g)[:, None, None] * HG + np.arange(HG)[None, None, :]))
    selc = jnp.asarray(sel, BF16)
    selr = jnp.asarray(sel.transpose(0, 2, 1), BF16)
    bias = dt_bias.reshape(g, 1, HG)
    biast = dt_bias.reshape(g, HG, 1)
    a = -jnp.exp(a_log.astype(F32))
    a_row = a.reshape(g, 1, HG)
    a_col = a.reshape(g, HG, 1)
    dsk = jnp.repeat(d_skip.astype(F32), SSD_HEAD_DIM).reshape(1, di)
    cb = conv_b.reshape(1, -1)
    xoff = di // W
    boff = 2 * di // NS
    coff = boff + g
    rowblk = lambda b, gi, t: b * nt + t
    kern = functools.partial(_ssd_kernel, n_chunks=tb // SSD_CHUNK)
    return pl.pallas_call(
        kern,
        grid=(bsz, g, nt),
        in_specs=[
            pl.BlockSpec((tb, W), lambda b, gi, t: (rowblk(b, gi, t), gi)),
            pl.BlockSpec((tb, W), lambda b, gi, t: (rowblk(b, gi, t), xoff + gi)),
            pl.BlockSpec((tb, NS), lambda b, gi, t: (rowblk(b, gi, t), boff + gi)),
            pl.BlockSpec((tb, NS), lambda b, gi, t: (rowblk(b, gi, t), coff + gi)),
            pl.BlockSpec((tb, LANES), lambda b, gi, t: (rowblk(b, gi, t), 0)),
            pl.BlockSpec((None, LANES, HG), lambda b, gi, t: (gi, 0, 0)),
            pl.BlockSpec((None, HG, LANES), lambda b, gi, t: (gi, 0, 0)),
            pl.BlockSpec((SSD_CONV, W), lambda b, gi, t: (0, gi)),
            pl.BlockSpec((SSD_CONV, NS), lambda b, gi, t: (0, di // NS + gi)),
            pl.BlockSpec((SSD_CONV, NS), lambda b, gi, t: (0, di // NS + g + gi)),
            pl.BlockSpec((1, W), lambda b, gi, t: (0, gi)),
            pl.BlockSpec((1, NS), lambda b, gi, t: (0, di // NS + gi)),
            pl.BlockSpec((1, NS), lambda b, gi, t: (0, di // NS + g + gi)),
            pl.BlockSpec((None, 1, HG), lambda b, gi, t: (gi, 0, 0)),
            pl.BlockSpec((None, HG, 1), lambda b, gi, t: (gi, 0, 0)),
            pl.BlockSpec((None, 1, HG), lambda b, gi, t: (gi, 0, 0)),
            pl.BlockSpec((None, HG, 1), lambda b, gi, t: (gi, 0, 0)),
            pl.BlockSpec((1, W), lambda b, gi, t: (0, gi)),
            pl.BlockSpec((1, W), lambda b, gi, t: (0, gi)),
        ],
        out_specs=pl.BlockSpec((tb, W), lambda b, gi, t: (rowblk(b, gi, t), gi)),
        out_shape=jax.ShapeDtypeStruct((n, di), BF16),
        scratch_shapes=[
            pltpu.VMEM((NS, W), F32),
            pltpu.VMEM((BF16_ROWS, W + 2 * NS), BF16),
        ],
        compiler_params=_cparams(("parallel", "parallel", "arbitrary")),
        name="ssd_scan",
    )(proj, proj, proj, proj, dtraw, selc, selr, conv_w, conv_w, conv_w, cb, cb, cb,
      bias, biast, a_row, a_col, dsk, norm_w.reshape(1, di))


def _gla_tables():
    c = GLA_CHUNK
    r = np.arange(c)
    d = np.zeros((GLA_LEVELS + 2, c, c), np.float32)
    masks = np.zeros((GLA_LEVELS + 1, c, c), np.float32)
    upper = np.zeros((GLA_LEVELS, c, 1), np.float32)
    masks[0] = np.eye(c)
    for j in range(1, GLA_LEVELS + 1):
        half = 1 << (j - 1)
        blk = r >> j
        mid = (blk << j) + half
        up = (r & ((1 << j) - 1)) >= half
        upper[j - 1, :, 0] = up
        for l in range(c):
            if up[l]:
                d[j - 1, l, mid[l]:l + 1] = 1.0
            else:
                d[j - 1, l, l + 1:mid[l]] = 1.0
        same = blk[:, None] == blk[None, :]
        masks[j] = same & up[:, None] & (~up)[None, :]
    d[GLA_LEVELS] = np.tril(np.ones((c, c)))
    d[GLA_LEVELS + 1] = np.triu(np.ones((c, c)), 1)
    return d.reshape(-1, c), masks, upper


def _gla_kernel(q_ref, k_ref, v_ref, g_ref, alr_ref, wa_ref, ba_ref, nw_ref,
                dall_ref, mask_ref, up_ref, o_ref, state_ref, *, n_chunks, hk):
    C = GLA_CHUNK
    scale = hk ** -0.5

    @pl.when(pl.program_id(2) == 0)
    def _():
        state_ref[...] = jnp.zeros_like(state_ref)

    wa = wa_ref[...]
    dall = dall_ref[...]

    a_hi, a_lo = _split_hi_lo(alr_ref[...])
    s = _dot(a_hi, wa)
    pre = s[:, :hk] + s[:, hk:] + _dot(a_lo, wa[:, :hk]) + ba_ref[...]
    lg = (jnp.minimum(pre, 0.0) - jnp.log1p(jnp.exp(-jnp.abs(pre)))) * (1.0 / GLA_GATE_TAU)
    lg_hi, lg_lo = _split_hi_lo(lg)
    lg2 = jnp.concatenate([lg_hi, lg_lo], axis=1)

    for ci in range(n_chunks):
        r0 = ci * C
        e2 = _dot(dall, lg2[r0:r0 + C, :])
        f = jnp.exp(e2[:, :hk] + e2[:, hk:])

        q = q_ref[r0:r0 + C, :].astype(F32) * scale
        k = k_ref[r0:r0 + C, :].astype(F32)
        v = v_ref[r0:r0 + C, :]
        sc = _dot_tb(q.astype(BF16), k.astype(BF16)) * mask_ref[0]
        for j in range(GLA_LEVELS):
            fj = f[j * C:(j + 1) * C, :]
            up = up_ref[j] > 0.5
            qj = jnp.where(up, q * fj, 0.0).astype(BF16)
            kj = jnp.where(up, 0.0, k * fj).astype(BF16)
            sc = sc + _dot_tb(qj, kj) * mask_ref[j + 1]
        f_cum = f[GLA_LEVELS * C:(GLA_LEVELS + 1) * C, :]
        f_end = f[(GLA_LEVELS + 1) * C:, :]
        state = state_ref[...]
        o = _dot(sc.astype(BF16), v) + _dot_tb((q * f_cum).astype(BF16), state.astype(BF16))
        state_ref[...] = state * f_cum[C - 1:C, :] + _dot_ta(v, (k * f_end).astype(BF16))

        on = o * lax.rsqrt(jnp.mean(o * o, axis=-1, keepdims=True) + EPS) * nw_ref[...]
        o_ref[r0:r0 + C, :] = (on * _silu(g_ref[r0:r0 + C, :].astype(F32))).astype(o_ref.dtype)


def gla_scan(proj, alr, w_a2, b_a, norm_w, bsz, tlen, tb):
    n = bsz * tlen
    dk = w_a2.shape[1]
    hk = dk // GLA_N_HEADS
    dv = (proj.shape[1] - 2 * dk) // 2
    hv = dv // GLA_N_HEADS
    nt = tlen // tb
    H = GLA_N_HEADS
    wa = jnp.zeros((LANES, dk), F32).at[:w_a2.shape[0], :].set(w_a2)
    wa_hi, wa_lo = _split_hi_lo(wa)
    wa_cat = jnp.concatenate([wa_hi.reshape(LANES, H, hk), wa_lo.reshape(LANES, H, hk)], axis=2)
    wa_cat = wa_cat.reshape(LANES, 2 * dk)
    d_all, masks, upper = _gla_tables()
    d_all = jnp.asarray(d_all, BF16)
    masks = jnp.asarray(masks, F32)
    upper = jnp.asarray(upper, F32)
    rowblk = lambda b, h, t: b * nt + t
    kern = functools.partial(_gla_kernel, n_chunks=tb // GLA_CHUNK, hk=hk)
    return pl.pallas_call(
        kern,
        grid=(bsz, H, nt),
        in_specs=[
            pl.BlockSpec((tb, hk), lambda b, h, t: (rowblk(b, h, t), h)),
            pl.BlockSpec((tb, hk), lambda b, h, t: (rowblk(b, h, t), H + h)),
            pl.BlockSpec((tb, hv), lambda b, h, t: (rowblk(b, h, t), 2 * dk // hv + h)),
            pl.BlockSpec((tb, hv), lambda b, h, t: (rowblk(b, h, t), 2 * dk // hv + H + h)),
            pl.BlockSpec((tb, LANES), lambda b, h, t: (rowblk(b, h, t), 0)),
            pl.BlockSpec((LANES, 2 * hk), lambda b, h, t: (0, h)),
            pl.BlockSpec((1, hk), lambda b, h, t: (0, h)),
            pl.BlockSpec((1, hv), lambda b, h, t: (0, 0)),
            pl.BlockSpec(d_all.shape, lambda b, h, t: (0, 0)),
            pl.BlockSpec(masks.shape, lambda b, h, t: (0, 0, 0)),
            pl.BlockSpec(upper.shape, lambda b, h, t: (0, 0, 0)),
        ],
        out_specs=pl.BlockSpec((tb, hv), lambda b, h, t: (rowblk(b, h, t), h)),
        out_shape=jax.ShapeDtypeStruct((n, dv), BF16),
        scratch_shapes=[pltpu.VMEM((hv, hk), F32)],
        compiler_params=_cparams(("parallel", "parallel", "arbitrary")),
        name="gla_scan",
    )(proj, proj, proj, proj, alr, wa_cat, b_a.reshape(1, dk), norm_w.reshape(1, hv),
      d_all, masks, upper)


def _route_kernel(lg_ref, bias_ref, ids_ref, gates_ref, cnt_ref, base_ref):
    tm = lg_ref.shape[0]
    G, EPG, E = MOE_GROUPS, MOE_EXPERTS_PER_GROUP, MOE_N_EXPERTS

    @pl.when(pl.program_id(0) == 0)
    def _():
        base_ref[...] = jnp.zeros_like(base_ref)

    lane = lax.broadcasted_iota(I32, (tm, LANES), 1)
    logits = lg_ref[...] + bias_ref[...]
    neg = jnp.float32(-jnp.inf)
    big = jnp.int32(LANES)
    is_g = lane < G
    lg_g = jnp.where(is_g, logits, neg)
    m_g = jnp.max(lg_g, axis=-1, keepdims=True)
    gi = jnp.min(jnp.where(lg_g == m_g, lane, big), axis=-1, keepdims=True)
    z_g = jnp.sum(jnp.where(is_g, jnp.exp(lg_g - m_g), 0.0), axis=-1, keepdims=True)
    grp_w = 1.0 / z_g
    in_grp = (lane >= G) & (lane < G + E) & (((lane - G) // EPG) == gi)
    lg_e = jnp.where(in_grp, logits, neg)
    m1 = jnp.max(lg_e, axis=-1, keepdims=True)
    l1 = jnp.min(jnp.where(lg_e == m1, lane, big), axis=-1, keepdims=True)
    lg_e2 = jnp.where(lane == l1, neg, lg_e)
    m2 = jnp.max(lg_e2, axis=-1, keepdims=True)
    l2 = jnp.min(jnp.where(lg_e2 == m2, lane, big), axis=-1, keepdims=True)
    p2 = jnp.exp(m2 - m1)
    g1 = grp_w / (1.0 + p2)
    g2 = grp_w * p2 / (1.0 + p2)
    e1 = l1 - G
    e2 = l2 - G
    onehot = jnp.where((lane == e1) | (lane == e2), 1.0, 0.0)
    r = lax.broadcasted_iota(I32, (tm, tm), 0)
    c = lax.broadcasted_iota(I32, (tm, tm), 1)
    strict = jnp.where(c < r, 1.0, 0.0).astype(BF16)
    cum = _dot(strict, onehot.astype(BF16)) + base_ref[0:1, :]
    rank1 = jnp.sum(jnp.where(lane == e1, cum, 0.0), axis=-1, keepdims=True).astype(I32)
    rank2 = jnp.sum(jnp.where(lane == e2, cum, 0.0), axis=-1, keepdims=True).astype(I32)
    total = base_ref[0:1, :] + jnp.sum(onehot, axis=0, keepdims=True)
    base_ref[...] = jnp.broadcast_to(total, base_ref.shape)
    cnt_ref[...] = jnp.broadcast_to(total, cnt_ref.shape).astype(I32)

    ids = jnp.where(lane == 0, e1, jnp.where(lane == 1, e2, jnp.where(lane == 2, rank1,
                    jnp.where(lane == 3, rank2, 0))))
    ids_ref[...] = ids
    gates_ref[...] = jnp.where(lane == 0, g1, jnp.where(lane == 1, g2, 0.0))


def moe_route(logits, bias, tm):
    n = logits.shape[0]
    return pl.pallas_call(
        _route_kernel,
        grid=(n // tm,),
        in_specs=[
            pl.BlockSpec((tm, LANES), lambda i: (i, 0)),
            pl.BlockSpec((1, LANES), lambda i: (0, 0)),
        ],
        out_specs=[
            pl.BlockSpec((tm, LANES), lambda i: (i, 0)),
            pl.BlockSpec((tm, LANES), lambda i: (i, 0)),
            pl.BlockSpec((8, LANES), lambda i: (0, 0)),
        ],
        out_shape=[
            jax.ShapeDtypeStruct((n, LANES), I32),
            jax.ShapeDtypeStruct((n, LANES), F32),
            jax.ShapeDtypeStruct((8, LANES), I32),
        ],
        scratch_shapes=[pltpu.VMEM((8, LANES), F32)],
        compiler_params=_cparams(("arbitrary",)),
        name="moe_route",
    )(logits, bias)


def _dispatch_kernel(zflag_ref, dest_ref, h_ref, xb_ref, zbuf_ref, sem, zsem, *, ns):
    tm = h_ref.shape[0] // ns
    blk = MOE_ROWS * ns
    n_blocks = xb_ref.shape[0] // blk

    @pl.when(pl.program_id(0) == 0)
    def _():
        zbuf_ref[...] = jnp.zeros_like(zbuf_ref)

        def zero_copy(b):
            r0 = pl.multiple_of(b * blk, blk)
            return pltpu.make_async_copy(zbuf_ref, xb_ref.at[pl.ds(r0, blk)], zsem)

        def zstart(b, carry):
            @pl.when(zflag_ref[b] > 0)
            def _():
                zero_copy(b).start()
            return carry

        def zwait(b, carry):
            @pl.when(zflag_ref[b] > 0)
            def _():
                zero_copy(b).wait()
            return carry

        lax.fori_loop(0, n_blocks, zstart, 0)
        lax.fori_loop(0, n_blocks, zwait, 0)

    def issue(r, carry):
        src = h_ref.at[pl.ds(pl.multiple_of(r * ns, ns), ns)]
        for s in range(MOE_TOP_K):
            d = pl.multiple_of(dest_ref[0, 0, MOE_TOP_K * r + s], ns)
            pltpu.make_async_copy(src, xb_ref.at[pl.ds(d, ns)], sem).start(priority=s % 2)
        return carry

    lax.fori_loop(0, tm, issue, 0, unroll=DMA_UNROLL)

    def drain(r, carry):
        for s in range(MOE_TOP_K):
            pltpu.make_async_copy(h_ref.at[pl.ds(0, ns)], xb_ref.at[pl.ds(0, ns)], sem).wait()
        return carry

    lax.fori_loop(0, tm, drain, 0, unroll=DMA_UNROLL)


def moe_dispatch(hp, dest, zflag, n_rows, ns, tm):
    n = hp.shape[0] // ns
    dest3 = dest.reshape(n // tm, 1, MOE_TOP_K * tm)
    return pl.pallas_call(
        functools.partial(_dispatch_kernel, ns=ns),
        grid=(n // tm,),
        in_specs=[
            pl.BlockSpec(memory_space=pltpu.SMEM),
            pl.BlockSpec((1, 1, MOE_TOP_K * tm), lambda i: (i, 0, 0), memory_space=pltpu.SMEM),
            pl.BlockSpec((tm * ns, LANES), lambda i: (i, 0)),
        ],
        out_specs=pl.BlockSpec(memory_space=pl.ANY),
        out_shape=jax.ShapeDtypeStruct((n_rows * ns, LANES), U32),
        scratch_shapes=[pltpu.VMEM((MOE_ROWS * ns, LANES), U32), pltpu.SemaphoreType.DMA(()),
                        pltpu.SemaphoreType.DMA(())],
        compiler_params=_cparams(("arbitrary",)),
        name="moe_dispatch",
    )(zflag, dest3, hp)


def _experts_kernel(be_ref, nb_ref, x_ref, wg_ref, wu_ref, wd_ref, y_ref, wgu_s, wd_s):
    i = pl.program_id(0)
    ff = wg_ref.shape[1]
    active = i < nb_ref[0]
    new_expert = (i == 0) | (be_ref[i] != be_ref[jnp.maximum(i - 1, 0)])

    @pl.when(active & new_expert)
    def _():
        wgu_s[:, :ff] = wg_ref[...].astype(BF16)
        wgu_s[:, ff:] = wu_ref[...].astype(BF16)
        wd_s[...] = wd_ref[...].astype(BF16)

    @pl.when(active)
    def _():
        xa, xb = _unpack_bf16_pair(_load_rows(x_ref, x_ref.shape[0] // MOE_ROWS))
        half = xa.shape[1]
        h = _dot(xa.astype(BF16), wgu_s[:half, :]) + _dot(xb.astype(BF16), wgu_s[half:, :])
        act = (_silu(h[:, :ff]) * h[:, ff:]).astype(BF16)
        y = _dot(act, wd_s[...])
        _store_rows(y_ref, _pack_bf16_pair(y[:, :half], y[:, half:]))

    @pl.when(jnp.logical_not(active))
    def _():
        y_ref[...] = jnp.zeros_like(y_ref)


def moe_experts(xbuf, block_e, n_used, w_gate, w_up, w_down, layer):
    d, ff = w_gate.shape[2], w_gate.shape[3]
    ns = d // 2 // LANES
    blk = MOE_ROWS * ns
    nb = xbuf.shape[0] // blk
    xmap = lambda i, be, nu: (jnp.minimum(i, nu[0] - 1), 0)
    wmap = lambda i, be, nu: (layer, be[i], 0, 0)
    grid_spec = pltpu.PrefetchScalarGridSpec(
        num_scalar_prefetch=2,
        grid=(nb,),
        in_specs=[
            pl.BlockSpec((blk, LANES), xmap),
            pl.BlockSpec((None, None, d, ff), wmap),
            pl.BlockSpec((None, None, d, ff), wmap),
            pl.BlockSpec((None, None, ff, d), wmap),
        ],
        out_specs=pl.BlockSpec((blk, LANES), lambda i, be, nu: (i, 0)),
        scratch_shapes=[pltpu.VMEM((d, 2 * ff), BF16), pltpu.VMEM((ff, d), BF16)],
    )
    return pl.pallas_call(
        _experts_kernel,
        grid_spec=grid_spec,
        out_shape=jax.ShapeDtypeStruct(xbuf.shape, U32),
        compiler_params=_cparams(("arbitrary",)),
        name="moe_experts",
    )(block_e, n_used, xbuf, w_gate, w_up, w_down)


def _combine_kernel(dest_ref, destn_ref, x_ref, gates_ref, nw_ref, yb_ref, o_ref, buf_ref, sem, *, final_norm):
    i = pl.program_id(0)
    tm = x_ref.shape[0]
    ns = buf_ref.shape[1] // tm
    slot = i % 2

    def issue_tile(dref, slot_):
        def issue(r, carry):
            r0 = pl.multiple_of(r * ns, ns)
            for s in range(MOE_TOP_K):
                d = pl.multiple_of(dref[0, 0, MOE_TOP_K * r + s], ns)
                pltpu.make_async_copy(yb_ref.at[pl.ds(d, ns)], buf_ref.at[slot_ * MOE_TOP_K + s, pl.ds(r0, ns)],
                                      sem.at[slot_]).start(priority=s % 2)
            return carry

        lax.fori_loop(0, tm, issue, 0, unroll=DMA_UNROLL)

    @pl.when(i == 0)
    def _():
        issue_tile(dest_ref, 0)

    @pl.when(i + 1 < pl.num_programs(0))
    def _():
        issue_tile(destn_ref, 1 - slot)

    def drain(r, carry):
        for s in range(MOE_TOP_K):
            pltpu.make_async_copy(yb_ref.at[pl.ds(0, ns)], buf_ref.at[0, pl.ds(0, ns)], sem.at[slot]).wait()
        return carry

    lax.fori_loop(0, tm, drain, 0, unroll=DMA_UNROLL)

    gates = gates_ref[...]
    a1, b1 = _unpack_bf16_pair(_load_rows(buf_ref.at[slot * MOE_TOP_K], ns))
    a2, b2 = _unpack_bf16_pair(_load_rows(buf_ref.at[slot * MOE_TOP_K + 1], ns))
    g1 = gates[:, 0:1]
    g2 = gates[:, 1:2]
    x = x_ref[...]
    half = a1.shape[1]
    xn = jnp.concatenate([x[:, :half] + g1 * a1 + g2 * a2, x[:, half:] + g1 * b1 + g2 * b2], axis=1)
    if final_norm:
        xn = xn * lax.rsqrt(jnp.mean(xn * xn, axis=-1, keepdims=True) + EPS) * nw_ref[...]
    o_ref[...] = xn


def moe_combine(x, ybuf, dest, gates, nw, tm, final_norm):
    n, d = x.shape
    ns = d // 2 // LANES
    dest3 = dest.reshape(n // tm, 1, MOE_TOP_K * tm)
    kern = functools.partial(_combine_kernel, final_norm=final_norm)
    nt = n // tm
    return pl.pallas_call(
        kern,
        grid=(nt,),
        in_specs=[
            pl.BlockSpec((1, 1, MOE_TOP_K * tm), lambda i: (i, 0, 0), memory_space=pltpu.SMEM),
            pl.BlockSpec((1, 1, MOE_TOP_K * tm), lambda i: (jnp.minimum(i + 1, nt - 1), 0, 0),
                         memory_space=pltpu.SMEM),
            pl.BlockSpec((tm, d), lambda i: (i, 0)),
            pl.BlockSpec((tm, LANES), lambda i: (i, 0)),
            pl.BlockSpec((1, d), lambda i: (0, 0)),
            pl.BlockSpec(memory_space=pl.ANY),
        ],
        out_specs=pl.BlockSpec((tm, d), lambda i: (i, 0)),
        out_shape=jax.ShapeDtypeStruct((n, d), F32),
        scratch_shapes=[pltpu.VMEM((2 * MOE_TOP_K, tm * ns, LANES), U32), pltpu.SemaphoreType.DMA((2,))],
        compiler_params=_cparams(("arbitrary",)),
        name="moe_combine",
    )(dest3, dest3, x, gates, nw.reshape(1, d), ybuf)


def hier_moe(x, hp, logits, b_group, b_expert, w_gate, w_up, w_down, layer, nw_final, final_norm, tiles):
    n = x.shape[0]
    E = MOE_N_EXPERTS
    bias = jnp.zeros((1, LANES), F32).at[0, :MOE_GROUPS].set(b_group).at[0, MOE_GROUPS:MOE_GROUPS + E].set(b_expert)
    ids, gates, cnt = moe_route(logits, bias, tiles["route"])
    counts = cnt[0, :E]
    pcounts = (counts + MOE_ROWS - 1) // MOE_ROWS * MOE_ROWS
    pends = jnp.cumsum(pcounts)
    poffs = pends - pcounts
    n_blocks = (n * MOE_TOP_K) // MOE_ROWS + E
    eidx = jnp.arange(E, dtype=I32)
    sel_off = jnp.sum(jnp.where(ids[:, 0:MOE_TOP_K, None] == eidx, poffs, 0), axis=-1)
    ns = x.shape[1] // 2 // LANES
    dest = (sel_off + ids[:, MOE_TOP_K:2 * MOE_TOP_K]) * ns
    blk_start = jnp.arange(n_blocks, dtype=I32) * MOE_ROWS
    n_used = (pends[-1] // MOE_ROWS).astype(I32)
    block_e = jnp.minimum(jnp.sum((pends[None, :] <= blk_start[:, None]).astype(I32), axis=1), E - 1)
    last_e = jnp.sum(jnp.where(jnp.arange(n_blocks) == n_used - 1, block_e, 0))
    block_e = jnp.where(jnp.arange(n_blocks) < n_used, block_e, last_e).astype(I32)
    blk_end = blk_start + MOE_ROWS
    is_last = jnp.any((pends[None, :] == blk_end[:, None]) & (counts[None, :] > 0), axis=1)
    zflag = (is_last | (jnp.arange(n_blocks) >= n_used)).astype(I32)
    xbuf = moe_dispatch(hp, dest, zflag, n_blocks * MOE_ROWS, ns, tiles["dispatch"])
    ybuf = moe_experts(xbuf, block_e, n_used.reshape(1), w_gate, w_up, w_down, layer)
    return moe_combine(x, ybuf, dest, gates, nw_final, tiles["combine"], final_norm)


def _tiles(n, tlen):
    pick = lambda pref, total: next(t for t in (pref, 512, 256, 128, 64) if t <= pref and total % t == 0)
    return {
        "mm_m": pick(1024, n), "mm_n": 1024, "out_m": pick(512, n),
        "ssd_t": pick(1024, tlen), "gla_t": pick(1024, tlen),
        "route": pick(512, n), "dispatch": pick(512, n), "combine": pick(512, n),
    }


def kernel(x, ssd_w_in, ssd_conv_w, ssd_conv_b, ssd_dt_bias, ssd_a_log, ssd_d, ssd_norm_w, ssd_w_out,
           gla_w_in, gla_w_a2, gla_b_a, gla_norm_w, gla_w_out, norm_mix, norm_ffn, moe_w_group,
           moe_b_group, moe_w_expert, moe_b_expert, moe_w_gate, moe_w_up, moe_w_down, norm_final):
    bsz, tlen, d = x.shape
    n = bsz * tlen
    depth = norm_mix.shape[0]
    tiles = _tiles(n, tlen)
    xf = x.reshape(n, d)
    ssd_w_in_b, ssd_w_out_b = ssd_w_in.astype(BF16), ssd_w_out.astype(BF16)
    gla_w_in_b, gla_w_out_b = gla_w_in.astype(BF16), gla_w_out.astype(BF16)
    for i in range(depth):
        j = i // 2
        if i % 2 == 0:
            m_main = ssd_w_in.shape[2] - ssd_dt_bias.shape[1]
            proj, dtraw = norm_matmul(xf, norm_mix[i], ssd_w_in_b, j, m_main,
                                      _small_weight(ssd_w_in[j, :, m_main:]), tiles["mm_m"], tiles["mm_n"])
            y = ssd_scan(proj, dtraw, ssd_conv_w[j], ssd_conv_b[j], ssd_dt_bias[j], ssd_a_log[j],
                         ssd_d[j], ssd_norm_w[j], bsz, tlen, tiles["ssd_t"])
            w_out = ssd_w_out_b
        else:
            m_main = gla_w_in.shape[2] - GLA_GATE_RANK
            proj, alr = norm_matmul(xf, norm_mix[i], gla_w_in_b, j, m_main,
                                    _small_weight(gla_w_in[j, :, m_main:]), tiles["mm_m"], tiles["mm_n"])
            y = gla_scan(proj, alr, gla_w_a2[j], gla_b_a[j], gla_norm_w[j], bsz, tlen, tiles["gla_t"])
            w_out = gla_w_out_b
        w_router = _small_weight(jnp.concatenate([moe_w_group[i], moe_w_expert[i]], axis=1))
        xf, hp, logits = out_proj(y, w_out, j, xf, norm_ffn[i], w_router, tiles["out_m"])
        xf = hier_moe(xf, hp, logits, moe_b_group[i], moe_b_expert[i], moe_w_gate, moe_w_up,
                      moe_w_down, i, norm_final, i == depth - 1, tiles)
    return xf.reshape(bsz, tlen, d)
```

```python
import functools

import numpy as np
import jax
import jax.numpy as jnp
from jax import lax
from jax.experimental import pallas as pl
from jax.experimental.pallas import tpu as pltpu

F32 = jnp.float32
BF16 = jnp.bfloat16
U32 = jnp.uint32
I32 = jnp.int32

EPS = 1e-6
LANES = 128
BF16_ROWS = 16
VMEM_LIMIT = 56 * 1024 * 1024

SSD_HEAD_DIM = 64
SSD_HEADS_PER_GROUP = 8
SSD_D_STATE = 128
SSD_CONV = 4
SSD_CHUNK = 128
SSD_GROUP_W = SSD_HEAD_DIM * SSD_HEADS_PER_GROUP

GLA_N_HEADS = 4
GLA_GATE_RANK = 16
GLA_GATE_TAU = 16.0
GLA_CHUNK = 128
GLA_LEVELS = 7

MOE_GROUPS = 8
MOE_EXPERTS_PER_GROUP = 8
MOE_N_EXPERTS = 64
MOE_TOP_K = 2
MOE_ROWS = 512
DMA_UNROLL = 8


def _cparams(sem):
    return pltpu.CompilerParams(dimension_semantics=sem, vmem_limit_bytes=VMEM_LIMIT)


def _split_hi_lo(v):
    hi = v.astype(BF16)
    lo = (v - hi.astype(F32)).astype(BF16)
    return hi, lo


def _dot(a, b):
    return jnp.dot(a, b, preferred_element_type=F32)


def _dot_tb(a, b):
    return lax.dot_general(a, b, (((1,), (1,)), ((), ())), preferred_element_type=F32)


def _dot_ta(a, b):
    return lax.dot_general(a, b, (((0,), (0,)), ((), ())), preferred_element_type=F32)


def _silu(v):
    hv = 0.5 * v
    return hv + hv * jnp.tanh(hv)


def _pack_bf16_pair(a, b):
    ua = lax.bitcast_convert_type(a.astype(BF16).astype(F32), U32)
    ub = lax.bitcast_convert_type(b.astype(BF16).astype(F32), U32)
    return ua | (ub >> 16)


def _unpack_bf16_pair(u):
    a = lax.bitcast_convert_type(u & jnp.uint32(0xFFFF0000), F32)
    b = lax.bitcast_convert_type(u << 16, F32)
    return a, b


def _store_rows(ref, v):
    ns = v.shape[1] // LANES
    for s in range(ns):
        ref[pl.ds(s, v.shape[0], stride=ns), :] = v[:, s * LANES:(s + 1) * LANES]


def _load_rows(ref, ns):
    rows = ref.shape[0] // ns
    return jnp.concatenate([ref[pl.ds(s, rows, stride=ns), :] for s in range(ns)], axis=1)


def _norm_matmul_kernel(x_ref, nw_ref, w_ref, ws_ref, o_ref, os_ref, h_ref):
    j = pl.program_id(1)

    @pl.when(j == 0)
    def _():
        x = x_ref[...]
        h = x * lax.rsqrt(jnp.mean(x * x, axis=-1, keepdims=True) + EPS) * nw_ref[...]
        hi = h.astype(BF16)
        h_ref[...] = hi
        s = _dot(hi, ws_ref[...])
        os_ref[...] = s[:, :LANES] + s[:, LANES:]

    o_ref[...] = _dot(h_ref[...], w_ref[...]).astype(o_ref.dtype)


def _small_weight(w):
    k, n = w.shape
    wp = jnp.zeros((k, LANES), F32).at[:, :n].set(w)
    hi, lo = _split_hi_lo(wp)
    return jnp.concatenate([hi, lo], axis=1)


def norm_matmul(x, nw, w_main, layer, m, w_small, tm, tn):
    n, d = x.shape
    return pl.pallas_call(
        _norm_matmul_kernel,
        grid=(n // tm, m // tn),
        in_specs=[
            pl.BlockSpec((tm, d), lambda i, j: (i, 0)),
            pl.BlockSpec((1, d), lambda i, j: (0, 0)),
            pl.BlockSpec((None, d, tn), lambda i, j: (layer, 0, j)),
            pl.BlockSpec((d, 2 * LANES), lambda i, j: (0, 0)),
        ],
        out_specs=[
            pl.BlockSpec((tm, tn), lambda i, j: (i, j)),
            pl.BlockSpec((tm, LANES), lambda i, j: (i, 0)),
        ],
        out_shape=[
            jax.ShapeDtypeStruct((n, m), BF16),
            jax.ShapeDtypeStruct((n, LANES), F32),
        ],
        scratch_shapes=[pltpu.VMEM((tm, d), BF16)],
        compiler_params=_cparams(("parallel", "arbitrary")),
        name="norm_matmul",
    )(x, nw.reshape(1, d), w_main, w_small)


def _out_proj_kernel(y_ref, w_ref, x_ref, nw_ref, wr_ref, xo_ref, hp_ref, lg_ref):
    xn = x_ref[...] + _dot(y_ref[...], w_ref[...])
    xo_ref[...] = xn
    h = xn * lax.rsqrt(jnp.mean(xn * xn, axis=-1, keepdims=True) + EPS) * nw_ref[...]
    half = h.shape[1] // 2
    _store_rows(hp_ref, _pack_bf16_pair(h[:, :half], h[:, half:]))
    hi, lo = _split_hi_lo(h)
    s = _dot(hi, wr_ref[...])
    lg_ref[...] = s[:, :LANES] + s[:, LANES:] + _dot(lo, wr_ref[:, :LANES])


def out_proj(y, w_out, layer, x, nw, w_router, tm):
    n, k = y.shape
    d = x.shape[1]
    return pl.pallas_call(
        _out_proj_kernel,
        grid=(n // tm,),
        in_specs=[
            pl.BlockSpec((tm, k), lambda i: (i, 0)),
            pl.BlockSpec((None, k, d), lambda i: (layer, 0, 0), pipeline_mode=pl.Buffered(1)),
            pl.BlockSpec((tm, d), lambda i: (i, 0)),
            pl.BlockSpec((1, d), lambda i: (0, 0)),
            pl.BlockSpec((d, 2 * LANES), lambda i: (0, 0)),
        ],
        out_specs=[
            pl.BlockSpec((tm, d), lambda i: (i, 0)),
            pl.BlockSpec((tm * (d // 2 // LANES), LANES), lambda i: (i, 0)),
            pl.BlockSpec((tm, LANES), lambda i: (i, 0)),
        ],
        out_shape=[
            jax.ShapeDtypeStruct((n, d), F32),
            jax.ShapeDtypeStruct((n * (d // 2 // LANES), LANES), U32),
            jax.ShapeDtypeStruct((n, LANES), F32),
        ],
        compiler_params=_cparams(("parallel",)),
        name="out_proj",
    )(y, w_out, x, nw.reshape(1, d), w_router)


def _ssd_kernel(z_ref, xs_ref, b_ref, c_ref, dt_ref, selc_ref, selr_ref,
                cwx_ref, cwb_ref, cwc_ref, cbx_ref, cbb_ref, cbc_ref,
                bias_ref, biast_ref, a_ref, at_ref, dsk_ref, nw_ref,
                y_ref, state_ref, tail_ref, *, n_chunks):
    L = SSD_CHUNK
    W = SSD_GROUP_W
    NS = SSD_D_STATE
    HG = SSD_HEADS_PER_GROUP
    P = SSD_HEAD_DIM
    TAIL = BF16_ROWS
    WC = W + 2 * NS
    tb = n_chunks * L

    @pl.when(pl.program_id(2) == 0)
    def _():
        state_ref[...] = jnp.zeros_like(state_ref)
        tail_ref[...] = jnp.zeros_like(tail_ref)

    row = lax.broadcasted_iota(I32, (L, L), 0)
    col = lax.broadcasted_iota(I32, (L, L), 1)
    causal = row >= col
    tri = jnp.where(causal, 1.0, 0.0).astype(BF16)
    trit = jnp.where(col >= row, 1.0, 0.0).astype(BF16)
    er = lax.broadcasted_iota(I32, (2 * HG, W), 0)
    ec = lax.broadcasted_iota(I32, (2 * HG, W), 1)
    expand = jnp.where((er % HG) == (ec // P), 1.0, 0.0).astype(BF16)
    sr = lax.broadcasted_iota(I32, ((SSD_CONV - 1) * L, L + TAIL), 0)
    sc = lax.broadcasted_iota(I32, ((SSD_CONV - 1) * L, L + TAIL), 1)
    shift = jnp.where(sc == (sr % L) + (sr // L) + (TAIL - SSD_CONV + 1), 1.0, 0.0).astype(BF16)

    cw = jnp.concatenate([cwx_ref[...], cwb_ref[...], cwc_ref[...]], axis=1)
    cb = jnp.concatenate([cbx_ref[...], cbb_ref[...], cbc_ref[...]], axis=1)
    a_row = a_ref[...]
    a_col = at_ref[...]

    def expand_heads(v):
        hi, lo = _split_hi_lo(v)
        return _dot(jnp.concatenate([hi, lo], axis=1), expand)

    d = dt_ref[...]
    d1 = d.astype(BF16)
    r1 = d - d1.astype(F32)
    d2 = r1.astype(BF16)
    d3 = (r1 - d2.astype(F32)).astype(BF16)
    dcat = jnp.concatenate([d1, d2, d3], axis=0)
    c3 = _dot(dcat, selc_ref[...])
    dt_all = jax.nn.softplus(c3[:tb] + c3[tb:2 * tb] + c3[2 * tb:] + bias_ref[...])
    r3 = _dot_tb(selr_ref[...], dcat)
    dtt_all = jax.nn.softplus(r3[:, :tb] + r3[:, tb:2 * tb] + r3[:, 2 * tb:] + biast_ref[...])

    u_all = jnp.concatenate([xs_ref[...], b_ref[...], c_ref[...]], axis=1)
    ucat = jnp.concatenate([tail_ref[...], u_all], axis=0)
    tail_ref[...] = u_all[tb - TAIL:, :]

    for ci in range(n_chunks):
        r0 = ci * L
        shifted = _dot(shift, ucat[r0:r0 + L + TAIL, :])
        acc = cb + u_all[r0:r0 + L, :].astype(F32) * cw[SSD_CONV - 1:SSD_CONV, :]
        for k in range(SSD_CONV - 1):
            acc = acc + shifted[k * L:(k + 1) * L, :] * cw[k:k + 1, :]
        conv = _silu(acc)
        xc = conv[:, :W]
        bc = conv[:, W:W + NS].astype(BF16)
        cc = conv[:, W + NS:].astype(BF16)

        dt = dt_all[r0:r0 + L, :]
        da_hi, da_lo = _split_hi_lo(dt * a_row)
        acs2 = _dot(tri, jnp.concatenate([da_hi, da_lo], axis=1))
        acs = acs2[:, :HG] + acs2[:, HG:]
        dat_hi, dat_lo = _split_hi_lo(dtt_all[:, r0:r0 + L] * a_col)
        acst2 = _dot(jnp.concatenate([dat_hi, dat_lo], axis=0), trit)
        acst = acst2[:HG, :] + acst2[HG:, :]

        acs_last = acs[L - 1:L, :]
        dt_e = expand_heads(dt)
        eacs_e = expand_heads(jnp.exp(acs))
        wend_e = expand_heads(dt * jnp.exp(acs_last - acs))
        xdt_b = (xc * dt_e).astype(BF16)

        cbm = _dot_tb(cc, bc)
        state = state_ref[...]
        y_off = _dot(cc, state.astype(BF16)) * eacs_e
        ys = []
        for h in range(HG):
            seg = acs[:, h:h + 1] - acst[h:h + 1, :]
            m = jnp.where(causal, jnp.exp(seg), 0.0) * cbm
            ys.append(_dot(m.astype(BF16), xdt_b[:, h * P:(h + 1) * P]))
        y = jnp.concatenate(ys, axis=1) + y_off + xc * dsk_ref[...]
        state_ref[...] = state * eacs_e[L - 1:L, :] + _dot_ta(bc, (xc * wend_e).astype(BF16))

        yg = y * _silu(z_ref[r0:r0 + L, :].astype(F32))
        yn = yg * lax.rsqrt(jnp.mean(yg * yg, axis=-1, keepdims=True) + EPS) * nw_ref[...]
        y_ref[r0:r0 + L, :] = yn.astype(y_ref.dtype)


def ssd_scan(proj, dtraw, conv_w, conv_b, dt_bias, a_log, d_skip, norm_w, bsz, tlen, tb):
    n = bsz * tlen
    n_heads = dt_bias.shape[0]
    g = n_heads // SSD_HEADS_PER_GROUP
    di = g * SSD_GROUP_W
    W, NS, HG = SSD_GROUP_W, SSD_D_STATE, SSD_HEADS_PER_GROUP
    nt = tlen // tb
    lane = np.arange(LANES)
    sel = (lane[None, :, None] == (np.arange(g)[:, None, None] * HG + np.arange(HG)[None, None, :]))
    selc = jnp.asarray(sel, BF16)
    selr = jnp.asarray(sel.transpose(0, 2, 1), BF16)
    bias = dt_bias.reshape(g, 1, HG)
    biast = dt_bias.reshape(g, HG, 1)
    a = -jnp.exp(a_log.astype(F32))
    a_row = a.reshape(g, 1, HG)
    a_col = a.reshape(g, HG, 1)
    dsk = jnp.repeat(d_skip.astype(F32), SSD_HEAD_DIM).reshape(1, di)
    cb = conv_b.reshape(1, -1)
    xoff = di // W
    boff = 2 * di // NS
    coff = boff + g
    rowblk = lambda b, gi, t: b * nt + t
    kern = functools.partial(_ssd_kernel, n_chunks=tb // SSD_CHUNK)
    return pl.pallas_call(
        kern,
        grid=(bsz, g, nt),
        in_specs=[
            pl.BlockSpec((tb, W), lambda b, gi, t: (rowblk(b, gi, t), gi)),
            pl.BlockSpec((tb, W), lambda b, gi, t: (rowblk(b, gi, t), xoff + gi)),
            pl.BlockSpec((tb, NS), lambda b, gi, t: (rowblk(b, gi, t), boff + gi)),
            pl.BlockSpec((tb, NS), lambda b, gi, t: (rowblk(b, gi, t), coff + gi)),
            pl.BlockSpec((tb, LANES), lambda b, gi, t: (rowblk(b, gi, t), 0)),
            pl.BlockSpec((None, LANES, HG), lambda b, gi, t: (gi, 0, 0)),
            pl.BlockSpec((None, HG, LANES), lambda b, gi, t: (gi, 0, 0)),
            pl.BlockSpec((SSD_CONV, W), lambda b, gi, t: (0, gi)),
            pl.BlockSpec((SSD_CONV, NS), lambda b, gi, t: (0, di // NS + gi)),
            pl.BlockSpec((SSD_CONV, NS), lambda b, gi, t: (0, di // NS + g + gi)),
            pl.BlockSpec((1, W), lambda b, gi, t: (0, gi)),
            pl.BlockSpec((1, NS), lambda b, gi, t: (0, di // NS + gi)),
            pl.BlockSpec((1, NS), lambda b, gi, t: (0, di // NS + g + gi)),
            pl.BlockSpec((None, 1, HG), lambda b, gi, t: (gi, 0, 0)),
            pl.BlockSpec((None, HG, 1), lambda b, gi, t: (gi, 0, 0)),
            pl.BlockSpec((None, 1, HG), lambda b, gi, t: (gi, 0, 0)),
            pl.BlockSpec((None, HG, 1), lambda b, gi, t: (gi, 0, 0)),
            pl.BlockSpec((1, W), lambda b, gi, t: (0, gi)),
            pl.BlockSpec((1, W), lambda b, gi, t: (0, gi)),
        ],
        out_specs=pl.BlockSpec((tb, W), lambda b, gi, t: (rowblk(b, gi, t), gi)),
        out_shape=jax.ShapeDtypeStruct((n, di), BF16),
        scratch_shapes=[
            pltpu.VMEM((NS, W), F32),
            pltpu.VMEM((BF16_ROWS, W + 2 * NS), BF16),
        ],
        compiler_params=_cparams(("parallel", "parallel", "arbitrary")),
        name="ssd_scan",
    )(proj, proj, proj, proj, dtraw, selc, selr, conv_w, conv_w, conv_w, cb, cb, cb,
      bias, biast, a_row, a_col, dsk, norm_w.reshape(1, di))


def _gla_tables():
    c = GLA_CHUNK
    r = np.arange(c)
    d = np.zeros((GLA_LEVELS + 2, c, c), np.float32)
    masks = np.zeros((GLA_LEVELS + 1, c, c), np.float32)
    upper = np.zeros((GLA_LEVELS, c, 1), np.float32)
    masks[0] = np.eye(c)
    for j in range(1, GLA_LEVELS + 1):
        half = 1 << (j - 1)
        blk = r >> j
        mid = (blk << j) + half
        up = (r & ((1 << j) - 1)) >= half
        upper[j - 1, :, 0] = up
        for l in range(c):
            if up[l]:
                d[j - 1, l, mid[l]:l + 1] = 1.0
            else:
                d[j - 1, l, l + 1:mid[l]] = 1.0
        same = blk[:, None] == blk[None, :]
        masks[j] = same & up[:, None] & (~up)[None, :]
    d[GLA_LEVELS] = np.tril(np.ones((c, c)))
    d[GLA_LEVELS + 1] = np.triu(np.ones((c, c)), 1)
    return d.reshape(-1, c), masks, upper


def _gla_kernel(q_ref, k_ref, v_ref, g_ref, alr_ref, wa_ref, ba_ref, nw_ref,
                dall_ref, mask_ref, up_ref, o_ref, state_ref, *, n_chunks, hk):
    C = GLA_CHUNK
    scale = hk ** -0.5

    @pl.when(pl.program_id(2) == 0)
    def _():
        state_ref[...] = jnp.zeros_like(state_ref)

    wa = wa_ref[...]
    dall = dall_ref[...]

    a_hi, a_lo = _split_hi_lo(alr_ref[...])
    s = _dot(a_hi, wa)
    pre = s[:, :hk] + s[:, hk:] + _dot(a_lo, wa[:, :hk]) + ba_ref[...]
    lg = (jnp.minimum(pre, 0.0) - jnp.log1p(jnp.exp(-jnp.abs(pre)))) * (1.0 / GLA_GATE_TAU)
    lg_hi, lg_lo = _split_hi_lo(lg)
    lg2 = jnp.concatenate([lg_hi, lg_lo], axis=1)

    for ci in range(n_chunks):
        r0 = ci * C
        e2 = _dot(dall, lg2[r0:r0 + C, :])
        f = jnp.exp(e2[:, :hk] + e2[:, hk:])

        q = q_ref[r0:r0 + C, :].astype(F32) * scale
        k = k_ref[r0:r0 + C, :].astype(F32)
        v = v_ref[r0:r0 + C, :]
        sc = _dot_tb(q.astype(BF16), k.astype(BF16)) * mask_ref[0]
        for j in range(GLA_LEVELS):
            fj = f[j * C:(j + 1) * C, :]
            up = up_ref[j] > 0.5
            qj = jnp.where(up, q * fj, 0.0).astype(BF16)
            kj = jnp.where(up, 0.0, k * fj).astype(BF16)
            sc = sc + _dot_tb(qj, kj) * mask_ref[j + 1]
        f_cum = f[GLA_LEVELS * C:(GLA_LEVELS + 1) * C, :]
        f_end = f[(GLA_LEVELS + 1) * C:, :]
        state = state_ref[...]
        o = _dot(sc.astype(BF16), v) + _dot_tb((q * f_cum).astype(BF16), state.astype(BF16))
        state_ref[...] = state * f_cum[C - 1:C, :] + _dot_ta(v, (k * f_end).astype(BF16))

        on = o * lax.rsqrt(jnp.mean(o * o, axis=-1, keepdims=True) + EPS) * nw_ref[...]
        o_ref[r0:r0 + C, :] = (on * _silu(g_ref[r0:r0 + C, :].astype(F32))).astype(o_ref.dtype)


def gla_scan(proj, alr, w_a2, b_a, norm_w, bsz, tlen, tb):
    n = bsz * tlen
    dk = w_a2.shape[1]
    hk = dk // GLA_N_HEADS
    dv = (proj.shape[1] - 2 * dk) // 2
    hv = dv // GLA_N_HEADS
    nt = tlen // tb
    H = GLA_N_HEADS
    wa = jnp.zeros((LANES, dk), F32).at[:w_a2.shape[0], :].set(w_a2)
    wa_hi, wa_lo = _split_hi_lo(wa)
    wa_cat = jnp.concatenate([wa_hi.reshape(LANES, H, hk), wa_lo.reshape(LANES, H, hk)], axis=2)
    wa_cat = wa_cat.reshape(LANES, 2 * dk)
    d_all, masks, upper = _gla_tables()
    d_all = jnp.asarray(d_all, BF16)
    masks = jnp.asarray(masks, F32)
    upper = jnp.asarray(upper, F32)
    rowblk = lambda b, h, t: b * nt + t
    kern = functools.partial(_gla_kernel, n_chunks=tb // GLA_CHUNK, hk=hk)
    return pl.pallas_call(
        kern,
        grid=(bsz, H, nt),
        in_specs=[
            pl.BlockSpec((tb, hk), lambda b, h, t: (rowblk(b, h, t), h)),
            pl.BlockSpec((tb, hk), lambda b, h, t: (rowblk(b, h, t), H + h)),
            pl.BlockSpec((tb, hv), lambda b, h, t: (rowblk(b, h, t), 2 * dk // hv + h)),
            pl.BlockSpec((tb, hv), lambda b, h, t: (rowblk(b, h, t), 2 * dk // hv + H + h)),
            pl.BlockSpec((tb, LANES), lambda b, h, t: (rowblk(b, h, t), 0)),
            pl.BlockSpec((LANES, 2 * hk), lambda b, h, t: (0, h)),
            pl.BlockSpec((1, hk), lambda b, h, t: (0, h)),
            pl.BlockSpec((1, hv), lambda b, h, t: (0, 0)),
            pl.BlockSpec(d_all.shape, lambda b, h, t: (0, 0)),
            pl.BlockSpec(masks.shape, lambda b, h, t: (0, 0, 0)),
            pl.BlockSpec(upper.shape, lambda b, h, t: (0, 0, 0)),
        ],
        out_specs=pl.BlockSpec((tb, hv), lambda b, h, t: (rowblk(b, h, t), h)),
        out_shape=jax.ShapeDtypeStruct((n, dv), BF16),
        scratch_shapes=[pltpu.VMEM((hv, hk), F32)],
        compiler_params=_cparams(("parallel", "parallel", "arbitrary")),
        name="gla_scan",
    )(proj, proj, proj, proj, alr, wa_cat, b_a.reshape(1, dk), norm_w.reshape(1, hv),
      d_all, masks, upper)


def _route_kernel(lg_ref, bias_ref, ids_ref, gates_ref, cnt_ref, base_ref):
    tm = lg_ref.shape[0]
    G, EPG, E = MOE_GROUPS, MOE_EXPERTS_PER_GROUP, MOE_N_EXPERTS

    @pl.when(pl.program_id(0) == 0)
    def _():
        base_ref[...] = jnp.zeros_like(base_ref)

    lane = lax.broadcasted_iota(I32, (tm, LANES), 1)
    logits = lg_ref[...] + bias_ref[...]
    neg = jnp.float32(-jnp.inf)
    big = jnp.int32(LANES)
    is_g = lane < G
    lg_g = jnp.where(is_g, logits, neg)
    m_g = jnp.max(lg_g, axis=-1, keepdims=True)
    gi = jnp.min(jnp.where(lg_g == m_g, lane, big), axis=-1, keepdims=True)
    z_g = jnp.sum(jnp.where(is_g, jnp.exp(lg_g - m_g), 0.0), axis=-1, keepdims=True)
    grp_w = 1.0 / z_g
    in_grp = (lane >= G) & (lane < G + E) & (((lane - G) // EPG) == gi)
    lg_e = jnp.where(in_grp, logits, neg)
    m1 = jnp.max(lg_e, axis=-1, keepdims=True)
    l1 = jnp.min(jnp.where(lg_e == m1, lane, big), axis=-1, keepdims=True)
    lg_e2 = jnp.where(lane == l1, neg, lg_e)
    m2 = jnp.max(lg_e2, axis=-1, keepdims=True)
    l2 = jnp.min(jnp.where(lg_e2 == m2, lane, big), axis=-1, keepdims=True)
    p2 = jnp.exp(m2 - m1)
    g1 = grp_w / (1.0 + p2)
    g2 = grp_w * p2 / (1.0 + p2)
    e1 = l1 - G
    e2 = l2 - G
    onehot = jnp.where((lane == e1) | (lane == e2), 1.0, 0.0)
    r = lax.broadcasted_iota(I32, (tm, tm), 0)
    c = lax.broadcasted_iota(I32, (tm, tm), 1)
    strict = jnp.where(c < r, 1.0, 0.0).astype(BF16)
    cum = _dot(strict, onehot.astype(BF16)) + base_ref[0:1, :]
    rank1 = jnp.sum(jnp.where(lane == e1, cum, 0.0), axis=-1, keepdims=True).astype(I32)
    rank2 = jnp.sum(jnp.where(lane == e2, cum, 0.0), axis=-1, keepdims=True).astype(I32)
    total = base_ref[0:1, :] + jnp.sum(onehot, axis=0, keepdims=True)
    base_ref[...] = jnp.broadcast_to(total, base_ref.shape)
    cnt_ref[...] = jnp.broadcast_to(total, cnt_ref.shape).astype(I32)

    ids = jnp.where(lane == 0, e1, jnp.where(lane == 1, e2, jnp.where(lane == 2, rank1,
                    jnp.where(lane == 3, rank2, 0))))
    ids_ref[...] = ids
    gates_ref[...] = jnp.where(lane == 0, g1, jnp.where(lane == 1, g2, 0.0))


def moe_route(logits, bias, tm):
    n = logits.shape[0]
    return pl.pallas_call(
        _route_kernel,
        grid=(n // tm,),
        in_specs=[
            pl.BlockSpec((tm, LANES), lambda i: (i, 0)),
            pl.BlockSpec((1, LANES), lambda i: (0, 0)),
        ],
        out_specs=[
            pl.BlockSpec((tm, LANES), lambda i: (i, 0)),
            pl.BlockSpec((tm, LANES), lambda i: (i, 0)),
            pl.BlockSpec((8, LANES), lambda i: (0, 0)),
        ],
        out_shape=[
            jax.ShapeDtypeStruct((n, LANES), I32),
            jax.ShapeDtypeStruct((n, LANES), F32),
            jax.ShapeDtypeStruct((8, LANES), I32),
        ],
        scratch_shapes=[pltpu.VMEM((8, LANES), F32)],
        compiler_params=_cparams(("arbitrary",)),
        name="moe_route",
    )(logits, bias)


def _dispatch_kernel(zflag_ref, dest_ref, h_ref, xb_ref, zbuf_ref, sem, zsem, *, ns):
    tm = h_ref.shape[0] // ns
    blk = MOE_ROWS * ns
    n_blocks = xb_ref.shape[0] // blk

    @pl.when(pl.program_id(0) == 0)
    def _():
        zbuf_ref[...] = jnp.zeros_like(zbuf_ref)

        def zero_copy(b):
            r0 = pl.multiple_of(b * blk, blk)
            return pltpu.make_async_copy(zbuf_ref, xb_ref.at[pl.ds(r0, blk)], zsem)

        def zstart(b, carry):
            @pl.when(zflag_ref[b] > 0)
            def _():
                zero_copy(b).start()
            return carry

        def zwait(b, carry):
            @pl.when(zflag_ref[b] > 0)
            def _():
                zero_copy(b).wait()
            return carry

        lax.fori_loop(0, n_blocks, zstart, 0)
        lax.fori_loop(0, n_blocks, zwait, 0)

    def issue(r, carry):
        src = h_ref.at[pl.ds(pl.multiple_of(r * ns, ns), ns)]
        for s in range(MOE_TOP_K):
            d = pl.multiple_of(dest_ref[0, 0, MOE_TOP_K * r + s], ns)
            pltpu.make_async_copy(src, xb_ref.at[pl.ds(d, ns)], sem).start(priority=s % 2)
        return carry

    lax.fori_loop(0, tm, issue, 0, unroll=DMA_UNROLL)

    def drain(r, carry):
        for s in range(MOE_TOP_K):
            pltpu.make_async_copy(h_ref.at[pl.ds(0, ns)], xb_ref.at[pl.ds(0, ns)], sem).wait()
        return carry

    lax.fori_loop(0, tm, drain, 0, unroll=DMA_UNROLL)


def moe_dispatch(hp, dest, zflag, n_rows, ns, tm):
    n = hp.shape[0] // ns
    dest3 = dest.reshape(n // tm, 1, MOE_TOP_K * tm)
    return pl.pallas_call(
        functools.partial(_dispatch_kernel, ns=ns),
        grid=(n // tm,),
        in_specs=[
            pl.BlockSpec(memory_space=pltpu.SMEM),
            pl.BlockSpec((1, 1, MOE_TOP_K * tm), lambda i: (i, 0, 0), memory_space=pltpu.SMEM),
            pl.BlockSpec((tm * ns, LANES), lambda i: (i, 0)),
        ],
        out_specs=pl.BlockSpec(memory_space=pl.ANY),
        out_shape=jax.ShapeDtypeStruct((n_rows * ns, LANES), U32),
        scratch_shapes=[pltpu.VMEM((MOE_ROWS * ns, LANES), U32), pltpu.SemaphoreType.DMA(()),
                        pltpu.SemaphoreType.DMA(())],
        compiler_params=_cparams(("arbitrary",)),
        name="moe_dispatch",
    )(zflag, dest3, hp)


def _experts_kernel(be_ref, nb_ref, x_ref, wg_ref, wu_ref, wd_ref, y_ref, wgu_s, wd_s):
    i = pl.program_id(0)
    ff = wg_ref.shape[1]
    active = i < nb_ref[0]
    new_expert = (i == 0) | (be_ref[i] != be_ref[jnp.maximum(i - 1, 0)])

    @pl.when(active & new_expert)
    def _():
        wgu_s[:, :ff] = wg_ref[...].astype(BF16)
        wgu_s[:, ff:] = wu_ref[...].astype(BF16)
        wd_s[...] = wd_ref[...].astype(BF16)

    @pl.when(active)
    def _():
        xa, xb = _unpack_bf16_pair(_load_rows(x_ref, x_ref.shape[0] // MOE_ROWS))
        half = xa.shape[1]
        h = _dot(xa.astype(BF16), wgu_s[:half, :]) + _dot(xb.astype(BF16), wgu_s[half:, :])
        act = (_silu(h[:, :ff]) * h[:, ff:]).astype(BF16)
        y = _dot(act, wd_s[...])
        _store_rows(y_ref, _pack_bf16_pair(y[:, :half], y[:, half:]))

    @pl.when(jnp.logical_not(active))
    def _():
        y_ref[...] = jnp.zeros_like(y_ref)


def moe_experts(xbuf, block_e, n_used, w_gate, w_up, w_down, layer):
    d, ff = w_gate.shape[2], w_gate.shape[3]
    ns = d // 2 // LANES
    blk = MOE_ROWS * ns
    nb = xbuf.shape[0] // blk
    xmap = lambda i, be, nu: (jnp.minimum(i, nu[0] - 1), 0)
    wmap = lambda i, be, nu: (layer, be[i], 0, 0)
    grid_spec = pltpu.PrefetchScalarGridSpec(
        num_scalar_prefetch=2,
        grid=(nb,),
        in_specs=[
            pl.BlockSpec((blk, LANES), xmap),
            pl.BlockSpec((None, None, d, ff), wmap),
            pl.BlockSpec((None, None, d, ff), wmap),
            pl.BlockSpec((None, None, ff, d), wmap),
        ],
        out_specs=pl.BlockSpec((blk, LANES), lambda i, be, nu: (i, 0)),
        scratch_shapes=[pltpu.VMEM((d, 2 * ff), BF16), pltpu.VMEM((ff, d), BF16)],
    )
    return pl.pallas_call(
        _experts_kernel,
        grid_spec=grid_spec,
        out_shape=jax.ShapeDtypeStruct(xbuf.shape, U32),
        compiler_params=_cparams(("arbitrary",)),
        name="moe_experts",
    )(block_e, n_used, xbuf, w_gate, w_up, w_down)


def _combine_kernel(dest_ref, destn_ref, x_ref, gates_ref, nw_ref, yb_ref, o_ref, buf_ref, sem, *, final_norm):
    i = pl.program_id(0)
    tm = x_ref.shape[0]
    ns = buf_ref.shape[1] // tm
    slot = i % 2

    def issue_tile(dref, slot_):
        def issue(r, carry):
            r0 = pl.multiple_of(r * ns, ns)
            for s in range(MOE_TOP_K):
                d = pl.multiple_of(dref[0, 0, MOE_TOP_K * r + s], ns)
                pltpu.make_async_copy(yb_ref.at[pl.ds(d, ns)], buf_ref.at[slot_ * MOE_TOP_K + s, pl.ds(r0, ns)],
                                      sem.at[slot_]).start(priority=s % 2)
            return carry

        lax.fori_loop(0, tm, issue, 0, unroll=DMA_UNROLL)

    @pl.when(i == 0)
    def _():
        issue_tile(dest_ref, 0)

    @pl.when(i + 1 < pl.num_programs(0))
    def _():
        issue_tile(destn_ref, 1 - slot)

    def drain(r, carry):
        for s in range(MOE_TOP_K):
            pltpu.make_async_copy(yb_ref.at[pl.ds(0, ns)], buf_ref.at[0, pl.ds(0, ns)], sem.at[slot]).wait()
        return carry

    lax.fori_loop(0, tm, drain, 0, unroll=DMA_UNROLL)

    gates = gates_ref[...]
    a1, b1 = _unpack_bf16_pair(_load_rows(buf_ref.at[slot * MOE_TOP_K], ns))
    a2, b2 = _unpack_bf16_pair(_load_rows(buf_ref.at[slot * MOE_TOP_K + 1], ns))
    g1 = gates[:, 0:1]
    g2 = gates[:, 1:2]
    x = x_ref[...]
    half = a1.shape[1]
    xn = jnp.concatenate([x[:, :half] + g1 * a1 + g2 * a2, x[:, half:] + g1 * b1 + g2 * b2], axis=1)
    if final_norm:
        xn = xn * lax.rsqrt(jnp.mean(xn * xn, axis=-1, keepdims=True) + EPS) * nw_ref[...]
    o_ref[...] = xn


def moe_combine(x, ybuf, dest, gates, nw, tm, final_norm):
    n, d = x.shape
    ns = d // 2 // LANES
    dest3 = dest.reshape(n // tm, 1, MOE_TOP_K * tm)
    kern = functools.partial(_combine_kernel, final_norm=final_norm)
    nt = n // tm
    return pl.pallas_call(
        kern,
        grid=(nt,),
        in_specs=[
            pl.BlockSpec((1, 1, MOE_TOP_K * tm), lambda i: (i, 0, 0), memory_space=pltpu.SMEM),
            pl.BlockSpec((1, 1, MOE_TOP_K * tm), lambda i: (jnp.minimum(i + 1, nt - 1), 0, 0),
                         memory_space=pltpu.SMEM),
            pl.BlockSpec((tm, d), lambda i: (i, 0)),
            pl.BlockSpec((tm, LANES), lambda i: (i, 0)),
            pl.BlockSpec((1, d), lambda i: (0, 0)),
            pl.BlockSpec(memory_space=pl.ANY),
        ],
        out_specs=pl.BlockSpec((tm, d), lambda i: (i, 0)),
        out_shape=jax.ShapeDtypeStruct((n, d), F32),
        scratch_shapes=[pltpu.VMEM((2 * MOE_TOP_K, tm * ns, LANES), U32), pltpu.SemaphoreType.DMA((2,))],
        compiler_params=_cparams(("arbitrary",)),
        name="moe_combine",
    )(dest3, dest3, x, gates, nw.reshape(1, d), ybuf)


def hier_moe(x, hp, logits, b_group, b_expert, w_gate, w_up, w_down, layer, nw_final, final_norm, tiles):
    n = x.shape[0]
    E = MOE_N_EXPERTS
    bias = jnp.zeros((1, LANES), F32).at[0, :MOE_GROUPS].set(b_group).at[0, MOE_GROUPS:MOE_GROUPS + E].set(b_expert)
    ids, gates, cnt = moe_route(logits, bias, tiles["route"])
    counts = cnt[0, :E]
    pcounts = (counts + MOE_ROWS - 1) // MOE_ROWS * MOE_ROWS
    pends = jnp.cumsum(pcounts)
    poffs = pends - pcounts
    n_blocks = (n * MOE_TOP_K) // MOE_ROWS + E
    eidx = jnp.arange(E, dtype=I32)
    sel_off = jnp.sum(jnp.where(ids[:, 0:MOE_TOP_K, None] == eidx, poffs, 0), axis=-1)
    ns = x.shape[1] // 2 // LANES
    dest = (sel_off + ids[:, MOE_TOP_K:2 * MOE_TOP_K]) * ns
    blk_start = jnp.arange(n_blocks, dtype=I32) * MOE_ROWS
    n_used = (pends[-1] // MOE_ROWS).astype(I32)
    block_e = jnp.minimum(jnp.sum((pends[None, :] <= blk_start[:, None]).astype(I32), axis=1), E - 1)
    last_e = jnp.sum(jnp.where(jnp.arange(n_blocks) == n_used - 1, block_e, 0))
    block_e = jnp.where(jnp.arange(n_blocks) < n_used, block_e, last_e).astype(I32)
    blk_end = blk_start + MOE_ROWS
    is_last = jnp.any((pends[None, :] == blk_end[:, None]) & (counts[None, :] > 0), axis=1)
    zflag = (is_last | (jnp.arange(n_blocks) >= n_used)).astype(I32)
    xbuf = moe_dispatch(hp, dest, zflag, n_blocks * MOE_ROWS, ns, tiles["dispatch"])
    ybuf = moe_experts(xbuf, block_e, n_used.reshape(1), w_gate, w_up, w_down, layer)
    return moe_combine(x, ybuf, dest, gates, nw_final, tiles["combine"], final_norm)


def _tiles(n, tlen):
    pick = lambda pref, total: next(t for t in (pref, 512, 256, 128, 64) if t <= pref and total % t == 0)
    return {
        "mm_m": pick(1024, n), "mm_n": 1024, "out_m": pick(512, n),
        "ssd_t": pick(1024, tlen), "gla_t": pick(1024, tlen),
        "route": pick(512, n), "dispatch": pick(1024, n), "combine": pick(1024, n),
    }


def kernel(x, ssd_w_in, ssd_conv_w, ssd_conv_b, ssd_dt_bias, ssd_a_log, ssd_d, ssd_norm_w, ssd_w_out,
           gla_w_in, gla_w_a2, gla_b_a, gla_norm_w, gla_w_out, norm_mix, norm_ffn, moe_w_group,
           moe_b_group, moe_w_expert, moe_b_expert, moe_w_gate, moe_w_up, moe_w_down, norm_final):
    bsz, tlen, d = x.shape
    n = bsz * tlen
    depth = norm_mix.shape[0]
    tiles = _tiles(n, tlen)
    xf = x.reshape(n, d)
    ssd_w_in_b, ssd_w_out_b = ssd_w_in.astype(BF16), ssd_w_out.astype(BF16)
    gla_w_in_b, gla_w_out_b = gla_w_in.astype(BF16), gla_w_out.astype(BF16)
    for i in range(depth):
        j = i // 2
        if i % 2 == 0:
            m_main = ssd_w_in.shape[2] - ssd_dt_bias.shape[1]
            proj, dtraw = norm_matmul(xf, norm_mix[i], ssd_w_in_b, j, m_main,
                                      _small_weight(ssd_w_in[j, :, m_main:]), tiles["mm_m"], tiles["mm_n"])
            y = ssd_scan(proj, dtraw, ssd_conv_w[j], ssd_conv_b[j], ssd_dt_bias[j], ssd_a_log[j],
                         ssd_d[j], ssd_norm_w[j], bsz, tlen, tiles["ssd_t"])
            w_out = ssd_w_out_b
        else:
            m_main = gla_w_in.shape[2] - GLA_GATE_RANK
            proj, alr = norm_matmul(xf, norm_mix[i], gla_w_in_b, j, m_main,
                                    _small_weight(gla_w_in[j, :, m_main:]), tiles["mm_m"], tiles["mm_n"])
            y = gla_scan(proj, alr, gla_w_a2[j], gla_b_a[j], gla_norm_w[j], bsz, tlen, tiles["gla_t"])
            w_out = gla_w_out_b
        w_router = _small_weight(jnp.concatenate([moe_w_group[i], moe_w_expert[i]], axis=1))
        xf, hp, logits = out_proj(y, w_out, j, xf, norm_ffn[i], w_router, tiles["out_m"])
        xf = hier_moe(xf, hp, logits, moe_b_group[i], moe_b_expert[i], moe_w_gate, moe_w_up,
                      moe_w_down, i, norm_final, i == depth - 1, tiles)
    return xf.reshape(bsz, tlen, d)
```

```python
import functools

import numpy as np
import jax
import jax.numpy as jnp
from jax import lax
from jax.experimental import pallas as pl
from jax.experimental.pallas import tpu as pltpu

F32 = jnp.float32
BF16 = jnp.bfloat16
U32 = jnp.uint32
I32 = jnp.int32

EPS = 1e-6
LANES = 128
BF16_ROWS = 16
VMEM_LIMIT = 56 * 1024 * 1024

SSD_HEAD_DIM = 64
SSD_HEADS_PER_GROUP = 8
SSD_D_STATE = 128
SSD_CONV = 4
SSD_CHUNK = 128
SSD_GROUP_W = SSD_HEAD_DIM * SSD_HEADS_PER_GROUP

GLA_N_HEADS = 4
GLA_GATE_RANK = 16
GLA_GATE_TAU = 16.0
GLA_CHUNK = 128
GLA_LEVELS = 7

MOE_GROUPS = 8
MOE_EXPERTS_PER_GROUP = 8
MOE_N_EXPERTS = 64
MOE_TOP_K = 2
MOE_ROWS = 512
DMA_UNROLL = 8


def _cparams(sem):
    return pltpu.CompilerParams(dimension_semantics=sem, vmem_limit_bytes=VMEM_LIMIT)


def _split_hi_lo(v):
    hi = v.astype(BF16)
    lo = (v - hi.astype(F32)).astype(BF16)
    return hi, lo


def _dot(a, b):
    return jnp.dot(a, b, preferred_element_type=F32)


def _dot_tb(a, b):
    return lax.dot_general(a, b, (((1,), (1,)), ((), ())), preferred_element_type=F32)


def _dot_ta(a, b):
    return lax.dot_general(a, b, (((0,), (0,)), ((), ())), preferred_element_type=F32)


def _silu(v):
    hv = 0.5 * v
    return hv + hv * jnp.tanh(hv)


def _pack_bf16_pair(a, b):
    ua = lax.bitcast_convert_type(a.astype(BF16).astype(F32), U32)
    ub = lax.bitcast_convert_type(b.astype(BF16).astype(F32), U32)
    return ua | (ub >> 16)


def _unpack_bf16_pair(u):
    a = lax.bitcast_convert_type(u & jnp.uint32(0xFFFF0000), F32)
    b = lax.bitcast_convert_type(u << 16, F32)
    return a, b


def _store_rows(ref, v):
    ns = v.shape[1] // LANES
    for s in range(ns):
        ref[pl.ds(s, v.shape[0], stride=ns), :] = v[:, s * LANES:(s + 1) * LANES]


def _load_rows(ref, ns):
    rows = ref.shape[0] // ns
    return jnp.concatenate([ref[pl.ds(s, rows, stride=ns), :] for s in range(ns)], axis=1)


def _norm_matmul_kernel(x_ref, nw_ref, w_ref, ws_ref, o_ref, os_ref, h_ref):
    j = pl.program_id(1)

    @pl.when(j == 0)
    def _():
        x = x_ref[...]
        h = x * lax.rsqrt(jnp.mean(x * x, axis=-1, keepdims=True) + EPS) * nw_ref[...]
        hi = h.astype(BF16)
        h_ref[...] = hi
        s = _dot(hi, ws_ref[...])
        os_ref[...] = s[:, :LANES] + s[:, LANES:]

    o_ref[...] = _dot(h_ref[...], w_ref[...]).astype(o_ref.dtype)


def _small_weight(w):
    k, n = w.shape
    wp = jnp.zeros((k, LANES), F32).at[:, :n].set(w)
    hi, lo = _split_hi_lo(wp)
    return jnp.concatenate([hi, lo], axis=1)


def norm_matmul(x, nw, w_main, layer, m, w_small, tm, tn):
    n, d = x.shape
    return pl.pallas_call(
        _norm_matmul_kernel,
        grid=(n // tm, m // tn),
        in_specs=[
            pl.BlockSpec((tm, d), lambda i, j: (i, 0)),
            pl.BlockSpec((1, d), lambda i, j: (0, 0)),
            pl.BlockSpec((None, d, tn), lambda i, j: (layer, 0, j)),
            pl.BlockSpec((d, 2 * LANES), lambda i, j: (0, 0)),
        ],
        out_specs=[
            pl.BlockSpec((tm, tn), lambda i, j: (i, j)),
            pl.BlockSpec((tm, LANES), lambda i, j: (i, 0)),
        ],
        out_shape=[
            jax.ShapeDtypeStruct((n, m), BF16),
            jax.ShapeDtypeStruct((n, LANES), F32),
        ],
        scratch_shapes=[pltpu.VMEM((tm, d), BF16)],
        compiler_params=_cparams(("parallel", "arbitrary")),
        name="norm_matmul",
    )(x, nw.reshape(1, d), w_main, w_small)


def _out_proj_kernel(y_ref, w_ref, x_ref, nw_ref, wr_ref, xo_ref, hp_ref, lg_ref):
    xn = x_ref[...] + _dot(y_ref[...], w_ref[...])
    xo_ref[...] = xn
    h = xn * lax.rsqrt(jnp.mean(xn * xn, axis=-1, keepdims=True) + EPS) * nw_ref[...]
    half = h.shape[1] // 2
    _store_rows(hp_ref, _pack_bf16_pair(h[:, :half], h[:, half:]))
    hi, lo = _split_hi_lo(h)
    s = _dot(hi, wr_ref[...])
    lg_ref[...] = s[:, :LANES] + s[:, LANES:] + _dot(lo, wr_ref[:, :LANES])


def out_proj(y, w_out, layer, x, nw, w_router, tm):
    n, k = y.shape
    d = x.shape[1]
    return pl.pallas_call(
        _out_proj_kernel,
        grid=(n // tm,),
        in_specs=[
            pl.BlockSpec((tm, k), lambda i: (i, 0)),
            pl.BlockSpec((None, k, d), lambda i: (layer, 0, 0), pipeline_mode=pl.Buffered(1)),
            pl.BlockSpec((tm, d), lambda i: (i, 0)),
            pl.BlockSpec((1, d), lambda i: (0, 0)),
            pl.BlockSpec((d, 2 * LANES), lambda i: (0, 0)),
        ],
        out_specs=[
            pl.BlockSpec((tm, d), lambda i: (i, 0)),
            pl.BlockSpec((tm * (d // 2 // LANES), LANES), lambda i: (i, 0)),
            pl.BlockSpec((tm, LANES), lambda i: (i, 0)),
        ],
        out_shape=[
            jax.ShapeDtypeStruct((n, d), F32),
            jax.ShapeDtypeStruct((n * (d // 2 // LANES), LANES), U32),
            jax.ShapeDtypeStruct((n, LANES), F32),
        ],
        compiler_params=_cparams(("parallel",)),
        name="out_proj",
    )(y, w_out, x, nw.reshape(1, d), w_router)


def _ssd_kernel(z_ref, xs_ref, b_ref, c_ref, dt_ref, selc_ref, selr_ref,
                cwx_ref, cwb_ref, cwc_ref, cbx_ref, cbb_ref, cbc_ref,
                bias_ref, biast_ref, a_ref, at_ref, dsk_ref, nw_ref,
                y_ref, state_ref, tail_ref, *, n_chunks):
    L = SSD_CHUNK
    W = SSD_GROUP_W
    NS = SSD_D_STATE
    HG = SSD_HEADS_PER_GROUP
    P = SSD_HEAD_DIM
    TAIL = BF16_ROWS
    WC = W + 2 * NS
    tb = n_chunks * L

    @pl.when(pl.program_id(2) == 0)
    def _():
        state_ref[...] = jnp.zeros_like(state_ref)
        tail_ref[...] = jnp.zeros_like(tail_ref)

    row = lax.broadcasted_iota(I32, (L, L), 0)
    col = lax.broadcasted_iota(I32, (L, L), 1)
    causal = row >= col
    tri = jnp.where(causal, 1.0, 0.0).astype(BF16)
    trit = jnp.where(col >= row, 1.0, 0.0).astype(BF16)
    er = lax.broadcasted_iota(I32, (2 * HG, W), 0)
    ec = lax.broadcasted_iota(I32, (2 * HG, W), 1)
    expand = jnp.where((er % HG) == (ec // P), 1.0, 0.0).astype(BF16)
    sr = lax.broadcasted_iota(I32, ((SSD_CONV - 1) * L, L + TAIL), 0)
    sc = lax.broadcasted_iota(I32, ((SSD_CONV - 1) * L, L + TAIL), 1)
    shift = jnp.where(sc == (sr % L) + (sr // L) + (TAIL - SSD_CONV + 1), 1.0, 0.0).astype(BF16)

    cw = jnp.concatenate([cwx_ref[...], cwb_ref[...], cwc_ref[...]], axis=1)
    cb = jnp.concatenate([cbx_ref[...], cbb_ref[...], cbc_ref[...]], axis=1)
    a_row = a_ref[...]
    a_col = at_ref[...]

    def expand_heads(v):
        hi, lo = _split_hi_lo(v)
        return _dot(jnp.concatenate([hi, lo], axis=1), expand)

    d = dt_ref[...]
    d1 = d.astype(BF16)
    r1 = d - d1.astype(F32)
    d2 = r1.astype(BF16)
    d3 = (r1 - d2.astype(F32)).astype(BF16)
    dcat = jnp.concatenate([d1, d2, d3], axis=0)
    c3 = _dot(dcat, selc_ref[...])
    dt_all = jax.nn.softplus(c3[:tb] + c3[tb:2 * tb] + c3[2 * tb:] + bias_ref[...])
    r3 = _dot_tb(selr_ref[...], dcat)
    dtt_all = jax.nn.softplus(r3[:, :tb] + r3[:, tb:2 * tb] + r3[:, 2 * tb:] + biast_ref[...])

    u_all = jnp.concatenate([xs_ref[...], b_ref[...], c_ref[...]], axis=1)
    ucat = jnp.concatenate([tail_ref[...], u_all], axis=0)
    tail_ref[...] = u_all[tb - TAIL:, :]

    for ci in range(n_chunks):
        r0 = ci * L
        shifted = _dot(shift, ucat[r0:r0 + L + TAIL, :])
        acc = cb + u_all[r0:r0 + L, :].astype(F32) * cw[SSD_CONV - 1:SSD_CONV, :]
        for k in range(SSD_CONV - 1):
            acc = acc + shifted[k * L:(k + 1) * L, :] * cw[k:k + 1, :]
        conv = _silu(acc)
        xc = conv[:, :W]
        bc = conv[:, W:W + NS].astype(BF16)
        cc = conv[:, W + NS:].astype(BF16)

        dt = dt_all[r0:r0 + L, :]
        da_hi, da_lo = _split_hi_lo(dt * a_row)
        acs2 = _dot(tri, jnp.concatenate([da_hi, da_lo], axis=1))
        acs = acs2[:, :HG] + acs2[:, HG:]
        dat_hi, dat_lo = _split_hi_lo(dtt_all[:, r0:r0 + L] * a_col)
        acst2 = _dot(jnp.concatenate([dat_hi, dat_lo], axis=0), trit)
        acst = acst2[:HG, :] + acst2[HG:, :]

        acs_last = acs[L - 1:L, :]
        dt_e = expand_heads(dt)
        eacs_e = expand_heads(jnp.exp(acs))
        wend_e = expand_heads(dt * jnp.exp(acs_last - acs))
        xdt_b = (xc * dt_e).astype(BF16)

        cbm = _dot_tb(cc, bc)
        state = state_ref[...]
        y_off = _dot(cc, state.astype(BF16)) * eacs_e
        ys = []
        for h in range(HG):
            seg = acs[:, h:h + 1] - acst[h:h + 1, :]
            m = jnp.where(causal, jnp.exp(seg), 0.0) * cbm
            ys.append(_dot(m.astype(BF16), xdt_b[:, h * P:(h + 1) * P]))
        y = jnp.concatenate(ys, axis=1) + y_off + xc * dsk_ref[...]
        state_ref[...] = state * eacs_e[L - 1:L, :] + _dot_ta(bc, (xc * wend_e).astype(BF16))

        yg = y * _silu(z_ref[r0:r0 + L, :].astype(F32))
        yn = yg * lax.rsqrt(jnp.mean(yg * yg, axis=-1, keepdims=True) + EPS) * nw_ref[...]
        y_ref[r0:r0 + L, :] = yn.astype(y_ref.dtype)


def ssd_scan(proj, dtraw, conv_w, conv_b, dt_bias, a_log, d_skip, norm_w, bsz, tlen, tb):
    n = bsz * tlen
    n_heads = dt_bias.shape[0]
    g = n_heads // SSD_HEADS_PER_GROUP
    di = g * SSD_GROUP_W
    W, NS, HG = SSD_GROUP_W, SSD_D_STATE, SSD_HEADS_PER_GROUP
    nt = tlen // tb
    lane = np.arange(LANES)
    sel = (lane[None, :, None] == (np.arange(g)[:, None, None] * HG + np.arange(HG)[None, None, :]))
    selc = jnp.asarray(sel, BF16)
    selr = jnp.asarray(sel.transpose(0, 2, 1), BF16)
    bias = dt_bias.reshape(g, 1, HG)
    biast = dt_bias.reshape(g, HG, 1)
    a = -jnp.exp(a_log.astype(F32))
    a_row = a.reshape(g, 1, HG)
    a_col = a.reshape(g, HG, 1)
    dsk = jnp.repeat(d_skip.astype(F32), SSD_HEAD_DIM).reshape(1, di)
    cb = conv_b.reshape(1, -1)
    xoff = di // W
    boff = 2 * di // NS
    coff = boff + g
    rowblk = lambda b, gi, t: b * nt + t
    kern = functools.partial(_ssd_kernel, n_chunks=tb // SSD_CHUNK)
    return pl.pallas_call(
        kern,
        grid=(bsz, g, nt),
        in_specs=[
            pl.BlockSpec((tb, W), lambda b, gi, t: (rowblk(b, gi, t), gi)),
            pl.BlockSpec((tb, W), lambda b, gi, t: (rowblk(b, gi, t), xoff + gi)),
            pl.BlockSpec((tb, NS), lambda b, gi, t: (rowblk(b, gi, t), boff + gi)),
            pl.BlockSpec((tb, NS), lambda b, gi, t: (rowblk(b, gi, t), coff + gi)),
            pl.BlockSpec((tb, LANES), lambda b, gi, t: (rowblk(b, gi, t), 0)),
            pl.BlockSpec((None, LANES, HG), lambda b, gi, t: (gi, 0, 0)),
            pl.BlockSpec((None, HG, LANES), lambda b, gi, t: (gi, 0, 0)),
            pl.BlockSpec((SSD_CONV, W), lambda b, gi, t: (0, gi)),
            pl.BlockSpec((SSD_CONV, NS), lambda b, gi, t: (0, di // NS + gi)),
            pl.BlockSpec((SSD_CONV, NS), lambda b, gi, t: (0, di // NS + g + gi)),
            pl.BlockSpec((1, W), lambda b, gi, t: (0, gi)),
            pl.BlockSpec((1, NS), lambda b, gi, t: (0, di // NS + gi)),
            pl.BlockSpec((1, NS), lambda b, gi, t: (0, di // NS + g + gi)),
            pl.BlockSpec((None, 1, HG), lambda b, gi, t: (gi, 0, 0)),
            pl.BlockSpec((None, HG, 1), lambda b, gi, t: (gi, 0, 0)),
            pl.BlockSpec((None, 1, HG), lambda b, gi, t: (gi, 0, 0)),
            pl.BlockSpec((None, HG, 1), lambda b, gi, t: (gi, 0, 0)),
            pl.BlockSpec((1, W), lambda b, gi, t: (0, gi)),
            pl.BlockSpec((1, W), lambda b, gi, t: (0, gi)),
        ],
        out_specs=pl.BlockSpec((tb, W), lambda b, gi, t: (rowblk(b, gi, t), gi)),
        out_shape=jax.ShapeDtypeStruct((n, di), BF16),
        scratch_shapes=[
            pltpu.VMEM((NS, W), F32),
            pltpu.VMEM((BF16_ROWS, W + 2 * NS), BF16),
        ],
        compiler_params=_cparams(("parallel", "parallel", "arbitrary")),
        name="ssd_scan",
    )(proj, proj, proj, proj, dtraw, selc, selr, conv_w, conv_w, conv_w, cb, cb, cb,
      bias, biast, a_row, a_col, dsk, norm_w.reshape(1, di))


def _gla_tables():
    c = GLA_CHUNK
    r = np.arange(c)
    d = np.zeros((GLA_LEVELS + 2, c, c), np.float32)
    masks = np.zeros((GLA_LEVELS + 1, c, c), np.float32)
    upper = np.zeros((GLA_LEVELS, c, 1), np.float32)
    masks[0] = np.eye(c)
    for j in range(1, GLA_LEVELS + 1):
        half = 1 << (j - 1)
        blk = r >> j
        mid = (blk << j) + half
        up = (r & ((1 << j) - 1)) >= half
        upper[j - 1, :, 0] = up
        for l in range(c):
            if up[l]:
                d[j - 1, l, mid[l]:l + 1] = 1.0
            else:
                d[j - 1, l, l + 1:mid[l]] = 1.0
        same = blk[:, None] == blk[None, :]
        masks[j] = same & up[:, None] & (~up)[None, :]
    d[GLA_LEVELS] = np.tril(np.ones((c, c)))
    d[GLA_LEVELS + 1] = np.triu(np.ones((c, c)), 1)
    return d.reshape(-1, c), masks, upper


def _gla_kernel(q_ref, k_ref, v_ref, g_ref, alr_ref, wa_ref, ba_ref, nw_ref,
                dall_ref, mask_ref, up_ref, o_ref, state_ref, *, n_chunks, hk):
    C = GLA_CHUNK
    scale = hk ** -0.5

    @pl.when(pl.program_id(2) == 0)
    def _():
        state_ref[...] = jnp.zeros_like(state_ref)

    wa = wa_ref[...]
    dall = dall_ref[...]

    a_hi, a_lo = _split_hi_lo(alr_ref[...])
    s = _dot(a_hi, wa)
    pre = s[:, :hk] + s[:, hk:] + _dot(a_lo, wa[:, :hk]) + ba_ref[...]
    lg = (jnp.minimum(pre, 0.0) - jnp.log1p(jnp.exp(-jnp.abs(pre)))) * (1.0 / GLA_GATE_TAU)
    lg_hi, lg_lo = _split_hi_lo(lg)
    lg2 = jnp.concatenate([lg_hi, lg_lo], axis=1)

    for ci in range(n_chunks):
        r0 = ci * C
        e2 = _dot(dall, lg2[r0:r0 + C, :])
        f = jnp.exp(e2[:, :hk] + e2[:, hk:])

        q = q_ref[r0:r0 + C, :].astype(F32) * scale
        k = k_ref[r0:r0 + C, :].astype(F32)
        v = v_ref[r0:r0 + C, :]
        sc = _dot_tb(q.astype(BF16), k.astype(BF16)) * mask_ref[0]
        for j in range(GLA_LEVELS):
            fj = f[j * C:(j + 1) * C, :]
            up = up_ref[j] > 0.5
            qj = jnp.where(up, q * fj, 0.0).astype(BF16)
            kj = jnp.where(up, 0.0, k * fj).astype(BF16)
            sc = sc + _dot_tb(qj, kj) * mask_ref[j + 1]
        f_cum = f[GLA_LEVELS * C:(GLA_LEVELS + 1) * C, :]
        f_end = f[(GLA_LEVELS + 1) * C:, :]
        state = state_ref[...]
        o = _dot(sc.astype(BF16), v) + _dot_tb((q * f_cum).astype(BF16), state.astype(BF16))
        state_ref[...] = state * f_cum[C - 1:C, :] + _dot_ta(v, (k * f_end).astype(BF16))

        on = o * lax.rsqrt(jnp.mean(o * o, axis=-1, keepdims=True) + EPS) * nw_ref[...]
        o_ref[r0:r0 + C, :] = (on * _silu(g_ref[r0:r0 + C, :].astype(F32))).astype(o_ref.dtype)


def gla_scan(proj, alr, w_a2, b_a, norm_w, bsz, tlen, tb):
    n = bsz * tlen
    dk = w_a2.shape[1]
    hk = dk // GLA_N_HEADS
    dv = (proj.shape[1] - 2 * dk) // 2
    hv = dv // GLA_N_HEADS
    nt = tlen // tb
    H = GLA_N_HEADS
    wa = jnp.zeros((LANES, dk), F32).at[:w_a2.shape[0], :].set(w_a2)
    wa_hi, wa_lo = _split_hi_lo(wa)
    wa_cat = jnp.concatenate([wa_hi.reshape(LANES, H, hk), wa_lo.reshape(LANES, H, hk)], axis=2)
    wa_cat = wa_cat.reshape(LANES, 2 * dk)
    d_all, masks, upper = _gla_tables()
    d_all = jnp.asarray(d_all, BF16)
    masks = jnp.asarray(masks, F32)
    upper = jnp.asarray(upper, F32)
    rowblk = lambda b, h, t: b * nt + t
    kern = functools.partial(_gla_kernel, n_chunks=tb // GLA_CHUNK, hk=hk)
    return pl.pallas_call(
        kern,
        grid=(bsz, H, nt),
        in_specs=[
            pl.BlockSpec((tb, hk), lambda b, h, t: (rowblk(b, h, t), h)),
            pl.BlockSpec((tb, hk), lambda b, h, t: (rowblk(b, h, t), H + h)),
            pl.BlockSpec((tb, hv), lambda b, h, t: (rowblk(b, h, t), 2 * dk // hv + h)),
            pl.BlockSpec((tb, hv), lambda b, h, t: (rowblk(b, h, t), 2 * dk // hv + H + h)),
            pl.BlockSpec((tb, LANES), lambda b, h, t: (rowblk(b, h, t), 0)),
            pl.BlockSpec((LANES, 2 * hk), lambda b, h, t: (0, h)),
            pl.BlockSpec((1, hk), lambda b, h, t: (0, h)),
            pl.BlockSpec((1, hv), lambda b, h, t: (0, 0)),
            pl.BlockSpec(d_all.shape, lambda b, h, t: (0, 0)),
            pl.BlockSpec(masks.shape, lambda b, h, t: (0, 0, 0)),
            pl.BlockSpec(upper.shape, lambda b, h, t: (0, 0, 0)),
        ],
        out_specs=pl.BlockSpec((tb, hv), lambda b, h, t: (rowblk(b, h, t), h)),
        out_shape=jax.ShapeDtypeStruct((n, dv), BF16),
        scratch_shapes=[pltpu.VMEM((hv, hk), F32)],
        compiler_params=_cparams(("parallel", "parallel", "arbitrary")),
        name="gla_scan",
    )(proj, proj, proj, proj, alr, wa_cat, b_a.reshape(1, dk), norm_w.reshape(1, hv),
      d_all, masks, upper)


def _route_kernel(lg_ref, bias_ref, ids_ref, gates_ref, cnt_ref, base_ref):
    tm = lg_ref.shape[0]
    G, EPG, E = MOE_GROUPS, MOE_EXPERTS_PER_GROUP, MOE_N_EXPERTS

    @pl.when(pl.program_id(0) == 0)
    def _():
        base_ref[...] = jnp.zeros_like(base_ref)

    lane = lax.broadcasted_iota(I32, (tm, LANES), 1)
    logits = lg_ref[...] + bias_ref[...]
    neg = jnp.float32(-jnp.inf)
    big = jnp.int32(LANES)
    is_g = lane < G
    lg_g = jnp.where(is_g, logits, neg)
    m_g = jnp.max(lg_g, axis=-1, keepdims=True)
    gi = jnp.min(jnp.where(lg_g == m_g, lane, big), axis=-1, keepdims=True)
    z_g = jnp.sum(jnp.where(is_g, jnp.exp(lg_g - m_g), 0.0), axis=-1, keepdims=True)
    grp_w = 1.0 / z_g
    in_grp = (lane >= G) & (lane < G + E) & (((lane - G) // EPG) == gi)
    lg_e = jnp.where(in_grp, logits, neg)
    m1 = jnp.max(lg_e, axis=-1, keepdims=True)
    l1 = jnp.min(jnp.where(lg_e == m1, lane, big), axis=-1, keepdims=True)
    lg_e2 = jnp.where(lane == l1, neg, lg_e)
    m2 = jnp.max(lg_e2, axis=-1, keepdims=True)
    l2 = jnp.min(jnp.where(lg_e2 == m2, lane, big), axis=-1, keepdims=True)
    p2 = jnp.exp(m2 - m1)
    g1 = grp_w / (1.0 + p2)
    g2 = grp_w * p2 / (1.0 + p2)
    e1 = l1 - G
    e2 = l2 - G
    onehot = jnp.where((lane == e1) | (lane == e2), 1.0, 0.0)
    r = lax.broadcasted_iota(I32, (tm, tm), 0)
    c = lax.broadcasted_iota(I32, (tm, tm), 1)
    strict = jnp.where(c < r, 1.0, 0.0).astype(BF16)
    cum = _dot(strict, onehot.astype(BF16)) + base_ref[0:1, :]
    rank1 = jnp.sum(jnp.where(lane == e1, cum, 0.0), axis=-1, keepdims=True).astype(I32)
    rank2 = jnp.sum(jnp.where(lane == e2, cum, 0.0), axis=-1, keepdims=True).astype(I32)
    total = base_ref[0:1, :] + jnp.sum(onehot, axis=0, keepdims=True)
    base_ref[...] = jnp.broadcast_to(total, base_ref.shape)
    cnt_ref[...] = jnp.broadcast_to(total, cnt_ref.shape).astype(I32)

    ids = jnp.where(lane == 0, e1, jnp.where(lane == 1, e2, jnp.where(lane == 2, rank1,
                    jnp.where(lane == 3, rank2, 0))))
    ids_ref[...] = ids
    gates_ref[...] = jnp.where(lane == 0, g1, jnp.where(lane == 1, g2, 0.0))


def moe_route(logits, bias, tm):
    n = logits.shape[0]
    return pl.pallas_call(
        _route_kernel,
        grid=(n // tm,),
        in_specs=[
            pl.BlockSpec((tm, LANES), lambda i: (i, 0)),
            pl.BlockSpec((1, LANES), lambda i: (0, 0)),
        ],
        out_specs=[
            pl.BlockSpec((tm, LANES), lambda i: (i, 0)),
            pl.BlockSpec((tm, LANES), lambda i: (i, 0)),
            pl.BlockSpec((8, LANES), lambda i: (0, 0)),
        ],
        out_shape=[
            jax.ShapeDtypeStruct((n, LANES), I32),
            jax.ShapeDtypeStruct((n, LANES), F32),
            jax.ShapeDtypeStruct((8, LANES), I32),
        ],
        scratch_shapes=[pltpu.VMEM((8, LANES), F32)],
        compiler_params=_cparams(("arbitrary",)),
        name="moe_route",
    )(logits, bias)


def _dispatch_kernel(zflag_ref, dest_ref, h_ref, xb_ref, zbuf_ref, sem, zsem, *, ns):
    tm = h_ref.shape[0] // ns
    blk = MOE_ROWS * ns
    n_blocks = xb_ref.shape[0] // blk

    @pl.when(pl.program_id(0) == 0)
    def _():
        zbuf_ref[...] = jnp.zeros_like(zbuf_ref)

        def zero_copy(b):
            r0 = pl.multiple_of(b * blk, blk)
            return pltpu.make_async_copy(zbuf_ref, xb_ref.at[pl.ds(r0, blk)], zsem)

        def zstart(b, carry):
            @pl.when(zflag_ref[b] > 0)
            def _():
                zero_copy(b).start()
            return carry

        def zwait(b, carry):
            @pl.when(zflag_ref[b] > 0)
            def _():
                zero_copy(b).wait()
            return carry

        lax.fori_loop(0, n_blocks, zstart, 0)
        lax.fori_loop(0, n_blocks, zwait, 0)

    def issue(r, carry):
        src = h_ref.at[pl.ds(pl.multiple_of(r * ns, ns), ns)]
        for s in range(MOE_TOP_K):
            d = pl.multiple_of(dest_ref[0, 0, MOE_TOP_K * r + s], ns)
            pltpu.make_async_copy(src, xb_ref.at[pl.ds(d, ns)], sem).start(priority=s % 2)
        return carry

    lax.fori_loop(0, tm, issue, 0, unroll=DMA_UNROLL)

    def drain(r, carry):
        for s in range(MOE_TOP_K):
            pltpu.make_async_copy(h_ref.at[pl.ds(0, ns)], xb_ref.at[pl.ds(0, ns)], sem).wait()
        return carry

    lax.fori_loop(0, tm, drain, 0, unroll=DMA_UNROLL)


def moe_dispatch(hp, dest, zflag, n_rows, ns, tm):
    n = hp.shape[0] // ns
    dest3 = dest.reshape(n // tm, 1, MOE_TOP_K * tm)
    return pl.pallas_call(
        functools.partial(_dispatch_kernel, ns=ns),
        grid=(n // tm,),
        in_specs=[
            pl.BlockSpec(memory_space=pltpu.SMEM),
            pl.BlockSpec((1, 1, MOE_TOP_K * tm), lambda i: (i, 0, 0), memory_space=pltpu.SMEM),
            pl.BlockSpec((tm * ns, LANES), lambda i: (i, 0)),
        ],
        out_specs=pl.BlockSpec(memory_space=pl.ANY),
        out_shape=jax.ShapeDtypeStruct((n_rows * ns, LANES), U32),
        scratch_shapes=[pltpu.VMEM((MOE_ROWS * ns, LANES), U32), pltpu.SemaphoreType.DMA(()),
                        pltpu.SemaphoreType.DMA(())],
        compiler_params=_cparams(("arbitrary",)),
        name="moe_dispatch",
    )(zflag, dest3, hp)


def _experts_kernel(be_ref, nb_ref, x_ref, wg_ref, wu_ref, wd_ref, y_ref, wgu_s, wd_s):
    i = pl.program_id(0)
    ff = wg_ref.shape[1]
    active = i < nb_ref[0]
    new_expert = (i == 0) | (be_ref[i] != be_ref[jnp.maximum(i - 1, 0)])

    @pl.when(active & new_expert)
    def _():
        wgu_s[:, :ff] = wg_ref[...].astype(BF16)
        wgu_s[:, ff:] = wu_ref[...].astype(BF16)
        wd_s[...] = wd_ref[...].astype(BF16)

    @pl.when(active)
    def _():
        xa, xb = _unpack_bf16_pair(_load_rows(x_ref, x_ref.shape[0] // MOE_ROWS))
        half = xa.shape[1]
        h = _dot(xa.astype(BF16), wgu_s[:half, :]) + _dot(xb.astype(BF16), wgu_s[half:, :])
        act = (_silu(h[:, :ff]) * h[:, ff:]).astype(BF16)
        y = _dot(act, wd_s[...])
        _store_rows(y_ref, _pack_bf16_pair(y[:, :half], y[:, half:]))

    @pl.when(jnp.logical_not(active))
    def _():
        y_ref[...] = jnp.zeros_like(y_ref)


def moe_experts(xbuf, block_e, n_used, w_gate, w_up, w_down, layer):
    d, ff = w_gate.shape[2], w_gate.shape[3]
    ns = d // 2 // LANES
    blk = MOE_ROWS * ns
    nb = xbuf.shape[0] // blk
    xmap = lambda i, be, nu: (jnp.minimum(i, nu[0] - 1), 0)
    wmap = lambda i, be, nu: (layer, be[i], 0, 0)
    grid_spec = pltpu.PrefetchScalarGridSpec(
        num_scalar_prefetch=2,
        grid=(nb,),
        in_specs=[
            pl.BlockSpec((blk, LANES), xmap),
            pl.BlockSpec((None, None, d, ff), wmap),
            pl.BlockSpec((None, None, d, ff), wmap),
            pl.BlockSpec((None, None, ff, d), wmap),
        ],
        out_specs=pl.BlockSpec((blk, LANES), lambda i, be, nu: (i, 0)),
        scratch_shapes=[pltpu.VMEM((d, 2 * ff), BF16), pltpu.VMEM((ff, d), BF16)],
    )
    return pl.pallas_call(
        _experts_kernel,
        grid_spec=grid_spec,
        out_shape=jax.ShapeDtypeStruct(xbuf.shape, U32),
        compiler_params=_cparams(("arbitrary",)),
        name="moe_experts",
    )(block_e, n_used, xbuf, w_gate, w_up, w_down)


def _combine_kernel(dest_ref, destn_ref, x_ref, gates_ref, nw_ref, yb_ref, o_ref, buf_ref, sem, *, final_norm):
    i = pl.program_id(0)
    tm = x_ref.shape[0]
    ns = buf_ref.shape[1] // tm
    slot = i % 2

    def issue_tile(dref, slot_):
        def issue(r, carry):
            r0 = pl.multiple_of(r * ns, ns)
            for s in range(MOE_TOP_K):
                d = pl.multiple_of(dref[0, 0, MOE_TOP_K * r + s], ns)
                pltpu.make_async_copy(yb_ref.at[pl.ds(d, ns)], buf_ref.at[slot_ * MOE_TOP_K + s, pl.ds(r0, ns)],
                                      sem.at[slot_]).start(priority=s % 2)
            return carry

        lax.fori_loop(0, tm, issue, 0, unroll=DMA_UNROLL)

    @pl.when(i == 0)
    def _():
        issue_tile(dest_ref, 0)

    @pl.when(i + 1 < pl.num_programs(0))
    def _():
        issue_tile(destn_ref, 1 - slot)

    def drain(r, carry):
        for s in range(MOE_TOP_K):
            pltpu.make_async_copy(yb_ref.at[pl.ds(0, ns)], buf_ref.at[0, pl.ds(0, ns)], sem.at[slot]).wait()
        return carry

    lax.fori_loop(0, tm, drain, 0, unroll=DMA_UNROLL)

    gates = gates_ref[...]
    a1, b1 = _unpack_bf16_pair(_load_rows(buf_ref.at[slot * MOE_TOP_K], ns))
    a2, b2 = _unpack_bf16_pair(_load_rows(buf_ref.at[slot * MOE_TOP_K + 1], ns))
    g1 = gates[:, 0:1]
    g2 = gates[:, 1:2]
    x = x_ref[...]
    half = a1.shape[1]
    xn = jnp.concatenate([x[:, :half] + g1 * a1 + g2 * a2, x[:, half:] + g1 * b1 + g2 * b2], axis=1)
    if final_norm:
        xn = xn * lax.rsqrt(jnp.mean(xn * xn, axis=-1, keepdims=True) + EPS) * nw_ref[...]
    o_ref[...] = xn


def moe_combine(x, ybuf, dest, gates, nw, tm, final_norm):
    n, d = x.shape
    ns = d // 2 // LANES
    dest3 = dest.reshape(n // tm, 1, MOE_TOP_K * tm)
    kern = functools.partial(_combine_kernel, final_norm=final_norm)
    nt = n // tm
    return pl.pallas_call(
        kern,
        grid=(nt,),
        in_specs=[
            pl.BlockSpec((1, 1, MOE_TOP_K * tm), lambda i: (i, 0, 0), memory_space=pltpu.SMEM),
            pl.BlockSpec((1, 1, MOE_TOP_K * tm), lambda i: (jnp.minimum(i + 1, nt - 1), 0, 0),
                         memory_space=pltpu.SMEM),
            pl.BlockSpec((tm, d), lambda i: (i, 0)),
            pl.BlockSpec((tm, LANES), lambda i: (i, 0)),
            pl.BlockSpec((1, d), lambda i: (0, 0)),
            pl.BlockSpec(memory_space=pl.ANY),
        ],
        out_specs=pl.BlockSpec((tm, d), lambda i: (i, 0)),
        out_shape=jax.ShapeDtypeStruct((n, d), F32),
        scratch_shapes=[pltpu.VMEM((2 * MOE_TOP_K, tm * ns, LANES), U32), pltpu.SemaphoreType.DMA((2,))],
        compiler_params=_cparams(("arbitrary",)),
        name="moe_combine",
    )(dest3, dest3, x, gates, nw.reshape(1, d), ybuf)


def hier_moe(x, hp, logits, b_group, b_expert, w_gate, w_up, w_down, layer, nw_final, final_norm, tiles):
    n = x.shape[0]
    E = MOE_N_EXPERTS
    bias = jnp.zeros((1, LANES), F32).at[0, :MOE_GROUPS].set(b_group).at[0, MOE_GROUPS:MOE_GROUPS + E].set(b_expert)
    ids, gates, cnt = moe_route(logits, bias, tiles["route"])
    counts = cnt[0, :E]
    pcounts = (counts + MOE_ROWS - 1) // MOE_ROWS * MOE_ROWS
    pends = jnp.cumsum(pcounts)
    poffs = pends - pcounts
    n_blocks = (n * MOE_TOP_K) // MOE_ROWS + E
    eidx = jnp.arange(E, dtype=I32)
    sel_off = jnp.sum(jnp.where(ids[:, 0:MOE_TOP_K, None] == eidx, poffs, 0), axis=-1)
    ns = x.shape[1] // 2 // LANES
    dest = (sel_off + ids[:, MOE_TOP_K:2 * MOE_TOP_K]) * ns
    blk_start = jnp.arange(n_blocks, dtype=I32) * MOE_ROWS
    n_used = (pends[-1] // MOE_ROWS).astype(I32)
    block_e = jnp.minimum(jnp.sum((pends[None, :] <= blk_start[:, None]).astype(I32), axis=1), E - 1)
    last_e = jnp.sum(jnp.where(jnp.arange(n_blocks) == n_used - 1, block_e, 0))
    block_e = jnp.where(jnp.arange(n_blocks) < n_used, block_e, last_e).astype(I32)
    blk_end = blk_start + MOE_ROWS
    is_last = jnp.any((pends[None, :] == blk_end[:, None]) & (counts[None, :] > 0), axis=1)
    zflag = (is_last | (jnp.arange(n_blocks) >= n_used)).astype(I32)
    xbuf = moe_dispatch(hp, dest, zflag, n_blocks * MOE_ROWS, ns, tiles["dispatch"])
    ybuf = moe_experts(xbuf, block_e, n_used.reshape(1), w_gate, w_up, w_down, layer)
    return moe_combine(x, ybuf, dest, gates, nw_final, tiles["combine"], final_norm)


def _tiles(n, tlen):
    pick = lambda pref, total: next(t for t in (pref, 512, 256, 128, 64) if t <= pref and total % t == 0)
    return {
        "mm_m": pick(1024, n), "mm_n": 1024, "out_m": pick(512, n),
        "ssd_t": pick(1024, tlen), "gla_t": pick(1024, tlen),
        "route": pick(512, n), "dispatch": pick(1024, n), "combine": pick(512, n),
    }


def kernel(x, ssd_w_in, ssd_conv_w, ssd_conv_b, ssd_dt_bias, ssd_a_log, ssd_d, ssd_norm_w, ssd_w_out,
           gla_w_in, gla_w_a2, gla_b_a, gla_norm_w, gla_w_out, norm_mix, norm_ffn, moe_w_group,
           moe_b_group, moe_w_expert, moe_b_expert, moe_w_gate, moe_w_up, moe_w_down, norm_final):
    bsz, tlen, d = x.shape
    n = bsz * tlen
    depth = norm_mix.shape[0]
    tiles = _tiles(n, tlen)
    xf = x.reshape(n, d)
    ssd_w_in_b, ssd_w_out_b = ssd_w_in.astype(BF16), ssd_w_out.astype(BF16)
    gla_w_in_b, gla_w_out_b = gla_w_in.astype(BF16), gla_w_out.astype(BF16)
    for i in range(depth):
        j = i // 2
        if i % 2 == 0:
            m_main = ssd_w_in.shape[2] - ssd_dt_bias.shape[1]
            proj, dtraw = norm_matmul(xf, norm_mix[i], ssd_w_in_b, j, m_main,
                                      _small_weight(ssd_w_in[j, :, m_main:]), tiles["mm_m"], tiles["mm_n"])
            y = ssd_scan(proj, dtraw, ssd_conv_w[j], ssd_conv_b[j], ssd_dt_bias[j], ssd_a_log[j],
                         ssd_d[j], ssd_norm_w[j], bsz, tlen, tiles["ssd_t"])
            w_out = ssd_w_out_b
        else:
            m_main = gla_w_in.shape[2] - GLA_GATE_RANK
            proj, alr = norm_matmul(xf, norm_mix[i], gla_w_in_b, j, m_main,
                                    _small_weight(gla_w_in[j, :, m_main:]), tiles["mm_m"], tiles["mm_n"])
            y = gla_scan(proj, alr, gla_w_a2[j], gla_b_a[j], gla_norm_w[j], bsz, tlen, tiles["gla_t"])
            w_out = gla_w_out_b
        w_router = _small_weight(jnp.concatenate([moe_w_group[i], moe_w_expert[i]], axis=1))
        xf, hp, logits = out_proj(y, w_out, j, xf, norm_ffn[i], w_router, tiles["out_m"])
        xf = hier_moe(xf, hp, logits, moe_b_group[i], moe_b_expert[i], moe_w_gate, moe_w_up,
                      moe_w_down, i, norm_final, i == depth - 1, tiles)
    return xf.reshape(bsz, tlen, d)
```
